```python
import jax, jax.numpy as jnp
from jax import lax
import numpy as np

D_MODEL = 1024
BATCH = 2
SEQ = 8192
DEPTH = 2

N_BRANCH = 4
BRANCH_W = D_MODEL // N_BRANCH
N_GROUPS = 4
GROUP_W = BRANCH_W // N_GROUPS
SHORTCONV_K = 3
GLA_CHUNK = 64
GLA_RANK = 16
GLA_TAU = 16.0
SGU_CHUNK = 128
LRU_CONV_K = 4
LRU_C = 8.0
N_MEM = 256
XA_HEADS = 4
XA_HEAD_DIM = D_MODEL // XA_HEADS
D_FF = ((8 * D_MODEL // 3 + 127) // 128) * 128
FFN_CONV_K = 3
N_NORMS = 7
EPS = 1e-6
IN_WIDTHS = (BRANCH_W, BRANCH_W, BRANCH_W,
             BRANCH_W, BRANCH_W, BRANCH_W, BRANCH_W, GLA_RANK,
             BRANCH_W, BRANCH_W,
             BRANCH_W, BRANCH_W)
D_IN = sum(IN_WIDTHS)
SPLIT_POINTS = tuple(int(v) for v in np.cumsum(IN_WIDTHS)[:-1])

kernel_name = 'hybrid_gated_parallel_mixer_trunk'


def rms_norm(x, g):
    xf = x.astype(jnp.float32)
    y = xf * lax.rsqrt(jnp.mean(xf * xf, axis=-1, keepdims=True) + EPS)
    return (y * g.astype(jnp.float32)).astype(x.dtype)


def layer_norm(x, g, b):
    xf = x.astype(jnp.float32)
    mu = jnp.mean(xf, axis=-1, keepdims=True)
    var = jnp.mean(jnp.square(xf - mu), axis=-1, keepdims=True)
    y = (xf - mu) * lax.rsqrt(var + EPS) * g.astype(jnp.float32) + b.astype(jnp.float32)
    return y.astype(x.dtype)


def causal_dwconv(x, w, b=None):
    k = w.shape[0]
    y = lax.conv_general_dilated(
        x, w[:, None, :].astype(x.dtype), window_strides=(1,), padding=[(k - 1, 0)],
        dimension_numbers=('NWC', 'WIO', 'NWC'), feature_group_count=x.shape[-1])
    if b is not None:
        y = y + b.astype(x.dtype)
    return y


def shortconv_branch(bg, cg, xin, conv_w):
    return bg * causal_dwconv(cg * xin, conv_w)


def gla_branch(q, k, v, r, a_lr, w_alpha, b_alpha, norm_g):
    bsz, s, _ = q.shape
    n = s // GLA_CHUNK

    def heads(t):
        return t.astype(jnp.float32).reshape(bsz, n, GLA_CHUNK, N_GROUPS, GROUP_W)

    qh = heads(q) * (GROUP_W ** -0.5)
    kh, vh = heads(k), heads(v)
    glog = jax.nn.log_sigmoid((a_lr @ w_alpha + b_alpha).astype(jnp.float32)) / GLA_TAU
    gcum = jnp.cumsum(heads(glog), axis=2)
    g_last = gcum[:, :, -1]
    q_dec = qh * jnp.exp(gcum)
    k_dec = kh * jnp.exp(-gcum)
    causal = jnp.tril(jnp.ones((GLA_CHUNK, GLA_CHUNK), dtype=bool))
    scores = jnp.where(causal, jnp.einsum('bnihd,bnjhd->bnhij', q_dec, k_dec), 0.0)
    o_intra = jnp.einsum('bnhij,bnjhv->bnihv', scores, vh)
    k_to_end = kh * jnp.exp(g_last[:, :, None] - gcum)
    kv = jnp.einsum('bnjhd,bnjhv->bnhdv', k_to_end, vh)

    def step(state, inp):
        decay, kv_n = inp
        return jnp.exp(decay)[..., None] * state + kv_n, state

    init = jnp.zeros((bsz, N_GROUPS, GROUP_W, GROUP_W), jnp.float32)
    _, s_prev = lax.scan(step, init, (jnp.moveaxis(g_last, 1, 0), jnp.moveaxis(kv, 1, 0)))
    s_prev = jnp.moveaxis(s_prev, 0, 1)
    o = o_intra + jnp.einsum('bnihd,bnhdv->bnihv', q_dec, s_prev)
    o = o * lax.rsqrt(jnp.mean(o * o, axis=-1, keepdims=True) + EPS)
    o = o.reshape(bsz, s, BRANCH_W) * norm_g.astype(jnp.float32)
    return o.astype(q.dtype) * jax.nn.silu(r)


def sgu_branch(u, v, ln_g, ln_b, w_s, b_s):
    bsz, s, _ = v.shape
    n = s // SGU_CHUNK
    vn = layer_norm(v, ln_g, ln_b).reshape(bsz, n, SGU_CHUNK, N_GROUPS, GROUP_W)
    mask = jnp.tril(jnp.ones((SGU_CHUNK, SGU_CHUNK), dtype=w_s.dtype))
    mixed = jnp.einsum('gij,bnjgc->bnigc', w_s * mask, vn) + b_s.T[:, :, None]
    return u * mixed.reshape(bsz, s, BRANCH_W)


def rglru_branch(xr, gate, conv_w, conv_b, w_a, b_a, w_x, b_x, lam):
    bsz, s, _ = xr.shape
    xc = causal_dwconv(xr, conv_w, conv_b)
    xg = xc.reshape(bsz, s, N_GROUPS, GROUP_W)
    r = jax.nn.sigmoid(jnp.einsum('bsgi,gio->bsgo', xg, w_a).reshape(bsz, s, BRANCH_W) + b_a)
    i = jax.nn.sigmoid(jnp.einsum('bsgi,gio->bsgo', xg, w_x).reshape(bsz, s, BRANCH_W) + b_x)
    log_a = -LRU_C * r.astype(jnp.float32) * jax.nn.softplus(-lam.astype(jnp.float32))
    a = jnp.exp(log_a)
    mult = jnp.sqrt(-jnp.expm1(2.0 * log_a))
    bx = mult * (i * xc).astype(jnp.float32)

    def combine(lft, rgt):
        a1, b1 = lft
        a2, b2 = rgt
        return a1 * a2, a2 * b1 + b2

    _, h = lax.associative_scan(combine, (a, bx), axis=1)
    return h.astype(xr.dtype) * jax.nn.gelu(gate)


def hybrid_mixer(h, w_in, sc_conv_w, gla_w_alpha, gla_b_alpha, gla_norm_g,
                 sgu_ln_g, sgu_ln_b, sgu_w, sgu_b,
                 lru_conv_w, lru_conv_b, lru_w_a, lru_b_a, lru_w_x, lru_b_x, lru_lambda,
                 w_gate, b_gate, w_branch, w_mix_out):
    bsz, s, _ = h.shape
    proj = h @ w_in
    (a_b, a_c, a_x, q, k, v, r, a_lr, su, sv, rx, rg) = jnp.split(proj, SPLIT_POINTS, axis=-1)
    ya = shortconv_branch(a_b, a_c, a_x, sc_conv_w)
    yb = gla_branch(q, k, v, r, a_lr, gla_w_alpha, gla_b_alpha, gla_norm_g)
    yc = sgu_branch(su, sv, sgu_ln_g, sgu_ln_b, sgu_w, sgu_b)
    yd = rglru_branch(rx, rg, lru_conv_w, lru_conv_b, lru_w_a, lru_b_a, lru_w_x, lru_b_x, lru_lambda)
    y_stack = jnp.stack([ya, yb, yc, yd], axis=2)
    branch = jnp.einsum('bskc,kcd->bskd', y_stack, w_branch)
    gates = jax.nn.sigmoid((h @ w_gate).reshape(bsz, s, N_BRANCH, D_MODEL) + b_gate)
    merged = jnp.sum(gates * branch, axis=2)
    return merged @ w_mix_out


def cross_attention(h, mem_n, wq, wkv, wo):
    bsz, s, _ = h.shape
    m = mem_n.shape[1]
    q = (h @ wq).reshape(bsz, s, XA_HEADS, XA_HEAD_DIM)
    k, v = jnp.split(mem_n @ wkv, 2, axis=-1)
    k = k.reshape(bsz, m, XA_HEADS, XA_HEAD_DIM)
    v = v.reshape(bsz, m, XA_HEADS, XA_HEAD_DIM)
    sc = jnp.einsum('bshd,bmhd->bhsm', q, k).astype(jnp.float32) * (XA_HEAD_DIM ** -0.5)
    p = jax.nn.softmax(sc, axis=-1).astype(v.dtype)
    o = jnp.einsum('bhsm,bmhd->bshd', p, v).reshape(bsz, s, D_MODEL)
    return o @ wo


def conv_ffn(h, w_up, conv_w, conv_b, w_down):
    up = causal_dwconv(h @ w_up, conv_w, conv_b)
    g, val = jnp.split(up, 2, axis=-1)
    return (jax.nn.gelu(g) * val) @ w_down


def setup_inputs(seed: int = 0) -> dict:
    key = jax.random.key(seed)
    ks = iter(jax.random.split(key, 40))
    L, D, W, G, C = DEPTH, D_MODEL, BRANCH_W, N_GROUPS, GROUP_W

    def nrm(shape, scale):
        return jax.random.normal(next(ks), shape, jnp.float32) * scale

    def gain(shape):
        return 1.0 + nrm(shape, 0.05)

    a8 = jax.random.uniform(next(ks), (L, W), jnp.float32, 0.9, 0.999)
    a_base = a8 ** (1.0 / LRU_C)
    lru_lambda = jnp.log(a_base) - jnp.log1p(-a_base)
    return {
        'x': nrm((BATCH, SEQ, D), 1.0),
        'mem': nrm((BATCH, N_MEM, D), 1.0),
        'norm_g': gain((L, N_NORMS, D)),
        'w_in': nrm((L, D, D_IN), D ** -0.5),
        'sc_conv_w': nrm((L, SHORTCONV_K, W), SHORTCONV_K ** -0.5),
        'gla_w_alpha': nrm((L, GLA_RANK, W), GLA_RANK ** -0.5),
        'gla_b_alpha': nrm((L, W), 0.5),
        'gla_norm_g': gain((L, W)),
        'sgu_ln_g': gain((L, W)),
        'sgu_ln_b': nrm((L, W), 0.02),
        'sgu_w': nrm((L, G, SGU_CHUNK, SGU_CHUNK), SGU_CHUNK ** -0.5),
        'sgu_b': 1.0 + nrm((L, G, SGU_CHUNK), 0.1),
        'lru_conv_w': nrm((L, LRU_CONV_K, W), LRU_CONV_K ** -0.5),
        'lru_conv_b': nrm((L, W), 0.02),
        'lru_w_a': nrm((L, G, C, C), C ** -0.5),
        'lru_b_a': nrm((L, W), 0.1),
        'lru_w_x': nrm((L, G, C, C), C ** -0.5),
        'lru_b_x': nrm((L, W), 0.1),
        'lru_lambda': lru_lambda,
        'w_gate': nrm((L, D, N_BRANCH * D), D ** -0.5),
        'b_gate': nrm((L, N_BRANCH, D), 0.1),
        'w_branch': nrm((L, N_BRANCH, W, D), W ** -0.5),
        'w_mix_out': nrm((L, D, D), D ** -0.5),
        'xa_wq': nrm((L, D, D), D ** -0.5),
        'xa_wkv': nrm((L, D, 2 * D), D ** -0.5),
        'xa_wo': nrm((L, D, D), D ** -0.5),
        'ffn_w_up': nrm((L, D, 2 * D_FF), D ** -0.5),
        'ffn_conv_w': nrm((L, FFN_CONV_K, 2 * D_FF), FFN_CONV_K ** -0.5),
        'ffn_conv_b': nrm((L, 2 * D_FF), 0.02),
        'ffn_w_down': nrm((L, D_FF, D), D_FF ** -0.5),
    }


def reference(x, mem, norm_g, w_in, sc_conv_w, gla_w_alpha, gla_b_alpha, gla_norm_g,
              sgu_ln_g, sgu_ln_b, sgu_w, sgu_b,
              lru_conv_w, lru_conv_b, lru_w_a, lru_b_a, lru_w_x, lru_b_x, lru_lambda,
              w_gate, b_gate, w_branch, w_mix_out, xa_wq, xa_wkv, xa_wo,
              ffn_w_up, ffn_conv_w, ffn_conv_b, ffn_w_down):
    for l in range(DEPTH):
        g = norm_g[l]
        h = rms_norm(x, g[0])
        y = hybrid_mixer(h, w_in[l], sc_conv_w[l], gla_w_alpha[l], gla_b_alpha[l], gla_norm_g[l],
                         sgu_ln_g[l], sgu_ln_b[l], sgu_w[l], sgu_b[l],
                         lru_conv_w[l], lru_conv_b[l], lru_w_a[l], lru_b_a[l], lru_w_x[l], lru_b_x[l],
                         lru_lambda[l], w_gate[l], b_gate[l], w_branch[l], w_mix_out[l])
        x = x + rms_norm(y, g[1])
        h = rms_norm(x, g[2])
        mem_n = rms_norm(mem, g[4])
        y = cross_attention(h, mem_n, xa_wq[l], xa_wkv[l], xa_wo[l])
        x = x + rms_norm(y, g[3])
        h = rms_norm(x, g[5])
        y = conv_ffn(h, ffn_w_up[l], ffn_conv_w[l], ffn_conv_b[l], ffn_w_down[l])
        x = x + rms_norm(y, g[6])
    return x
```

```python
import functools

import jax
import jax.numpy as jnp
from jax import lax
from jax.experimental import pallas as pl
from jax.experimental.pallas import tpu as pltpu

BF = jnp.bfloat16
F32 = jnp.float32

D_MODEL = 1024
BRANCH_W = 256
N_GROUPS = 4
GROUP_W = 64
GLA_CHUNK = 64
GLA_RANK = 16
GLA_TAU = 16.0
SGU_CHUNK = 128
LRU_C = 8.0
N_MEM = 256
XA_HEADS = 4
XA_HEAD_DIM = D_MODEL // XA_HEADS
D_FF = 2816
EPS = 1e-6
N_MAIN = 11 * BRANCH_W
LR_PAD = 128
D_IN_PAD = N_MAIN + LR_PAD

SEQ_TILE = 256
ROW_TILE = 256
FF_CHUNK = 256
VMEM_LIMIT = 56 * 1024 * 1024


def _dot(a, b):
    return jnp.dot(a, b, preferred_element_type=F32)


def _dot_nt(a, b):
    return lax.dot_general(a, b, (((1,), (1,)), ((), ())), preferred_element_type=F32)


def _rms(x, g):
    ms = jnp.mean(x * x, axis=-1, keepdims=True)
    return x * lax.rsqrt(ms + EPS) * g


def _gelu_tanh(x):
    return 0.5 * x * (1.0 + jnp.tanh(0.7978845608028654 * (x + 0.044715 * (x * x * x))))


def _softplus(x):
    return jnp.maximum(x, 0.0) + jnp.log1p(jnp.exp(-jnp.abs(x)))


def _shift_rows(z, tail8, k):
    zr = pltpu.roll(z, k, axis=0)
    cr = pltpu.roll(tail8, k, axis=0)
    row = lax.broadcasted_iota(jnp.int32, tail8.shape, 0)
    head = jnp.where(row < k, cr, zr[:8])
    return jnp.concatenate([head, zr[8:]], axis=0)


def _shift_rows_fill(z, k, fill):
    n, w = z.shape
    if k % 8 == 0:
        return jnp.concatenate([jnp.full((k, w), fill, z.dtype), z[: n - k]], axis=0)
    zr = pltpu.roll(z, k, axis=0)
    row = lax.broadcasted_iota(jnp.int32, (8, w), 0)
    head = jnp.where(row < k, jnp.full((8, w), fill, z.dtype), zr[:8])
    return jnp.concatenate([head, zr[8:]], axis=0)


def _hi_lo(x):
    hi = x.astype(BF)
    lo = (x - hi.astype(F32)).astype(BF)
    return hi, lo


def _mixer_kernel(x_ref, g0_ref, win_ref, scw_ref, wal_ref, bal_ref, gng_ref, lng_ref, lnb_ref,
                  sgw_ref, sgb_ref, lcw_ref, lcb_ref, wa_ref, ba_ref, wx_ref, bx_ref, lam_ref,
                  y_ref, zc_ref, rc_ref, st_ref, hc_ref):
    ts = SEQ_TILE
    w = BRANCH_W

    @pl.when(pl.program_id(1) == 0)
    def _():
        zc_ref[...] = jnp.zeros_like(zc_ref)
        rc_ref[...] = jnp.zeros_like(rc_ref)
        st_ref[...] = jnp.zeros_like(st_ref)
        hc_ref[...] = jnp.zeros_like(hc_ref)

    x = x_ref[0]
    h = _rms(x, g0_ref[...]).astype(BF)
    proj = _dot(h, win_ref[...])

    def part(i):
        return proj[:, w * i: w * (i + 1)]

    lane = lax.broadcasted_iota(jnp.int32, (1, w), 1)
    lane_grp = lane >> 6

    z = part(1) * part(2)
    zc = zc_ref[...]
    scw = scw_ref[...]
    conv = scw[2:3] * z + scw[1:2] * _shift_rows(z, zc, 1) + scw[0:1] * _shift_rows(z, zc, 2)
    ya = part(0) * conv
    zc_ref[...] = z[ts - 8:]

    q, k, v, r = part(3), part(4), part(5), part(6)
    alr = proj[:, N_MAIN:].astype(BF)
    logit = _dot(alr, wal_ref[...]) + bal_ref[...]
    glog = (jnp.minimum(logit, 0.0) - jnp.log1p(jnp.exp(-jnp.abs(logit)))) * (1.0 / GLA_TAU)
    ri = lax.broadcasted_iota(jnp.int32, (ts, ts), 0)
    ci = lax.broadcasted_iota(jnp.int32, (ts, ts), 1)
    same_chunk = (ri >> 6) == (ci >> 6)
    l_incl = jnp.where(same_chunk & (ci <= ri), 1.0, 0.0).astype(BF)
    u_strict = jnp.where(same_chunk & (ci > ri), 1.0, 0.0).astype(BF)
    g_hi, g_lo = _hi_lo(glog)
    gcum = _dot(l_incl, g_hi) + _dot(l_incl, g_lo)
    grev = _dot(u_strict, g_hi) + _dot(u_strict, g_lo)
    qd = q * (GROUP_W ** -0.5) * jnp.exp(gcum)
    kd = (k * jnp.exp(-gcum)).astype(BF)
    kte = (k * jnp.exp(grev)).astype(BF)
    vb = v.astype(BF)
    qd_b = qd.astype(BF)
    zero_b = jnp.zeros_like(qd_b)
    q_exp = jnp.concatenate([jnp.where(lane_grp == hh, qd_b, zero_b) for hh in range(N_GROUPS)], axis=0)
    s_exp = _dot_nt(q_exp, kd)
    r4 = lax.broadcasted_iota(jnp.int32, (N_GROUPS * ts, ts), 0) & (ts - 1)
    c4 = lax.broadcasted_iota(jnp.int32, (N_GROUPS * ts, ts), 1)
    s_exp = jnp.where(((r4 >> 6) == (c4 >> 6)) & (c4 <= r4), s_exp, 0.0)
    o_exp = _dot(s_exp.astype(BF), vb)
    o = jnp.zeros((ts, w), F32)
    for hh in range(N_GROUPS):
        o = o + jnp.where(lane_grp == hh, o_exp[hh * ts:(hh + 1) * ts], 0.0)
    vt = v.T.astype(BF)
    bd = (lax.broadcasted_iota(jnp.int32, (w, w), 0) >> 6) == (lax.broadcasted_iota(jnp.int32, (w, w), 1) >> 6)
    st = st_ref[...]
    n_chunks = ts // GLA_CHUNK
    o_inter = []
    zero_kte = jnp.zeros_like(kte)
    row_chunk = lax.broadcasted_iota(jnp.int32, (ts, w), 0) >> 6
    for c in range(n_chunks):
        lo_r, hi_r = c * GLA_CHUNK, (c + 1) * GLA_CHUNK
        o_inter.append(_dot_nt(qd_b[lo_r:hi_r], st.astype(BF)))
        kv_t = _dot(vt, jnp.where(row_chunk == c, kte, zero_kte))
        decay = jnp.exp(gcum[hi_r - 1:hi_r])
        st = st * decay + jnp.where(bd, kv_t, 0.0)
    st_ref[...] = st
    o = o + jnp.concatenate(o_inter, axis=0)
    bd_mean = jnp.where(bd, 1.0 / GROUP_W, 0.0).astype(BF)
    ms = _dot((o * o).astype(BF), bd_mean)
    o = o * lax.rsqrt(ms + EPS) * gng_ref[...]
    yb = o * (r * jax.nn.sigmoid(r))

    su, sv = part(7), part(8)
    mu = jnp.mean(sv, axis=-1, keepdims=True)
    svc = sv - mu
    var = jnp.mean(svc * svc, axis=-1, keepdims=True)
    vn = (svc * lax.rsqrt(var + EPS) * lng_ref[...] + lnb_ref[...]).astype(BF)
    wi = lax.broadcasted_iota(jnp.int32, (SGU_CHUNK, N_GROUPS * SGU_CHUNK), 0)
    wj = lax.broadcasted_iota(jnp.int32, (SGU_CHUNK, N_GROUPS * SGU_CHUNK), 1) & (SGU_CHUNK - 1)
    sgw = sgw_ref[...]
    w_mask = jnp.where(wj <= wi, sgw, jnp.zeros_like(sgw))
    zero_vn = jnp.zeros((SGU_CHUNK, w), BF)
    yc_parts = []
    for c in range(ts // SGU_CHUNK):
        vc = vn[c * SGU_CHUNK:(c + 1) * SGU_CHUNK]
        v_exp = jnp.concatenate([jnp.where(lane_grp == gg, vc, zero_vn) for gg in range(N_GROUPS)], axis=0)
        mixed = _dot(w_mask, v_exp) + sgb_ref[...]
        yc_parts.append(su[c * SGU_CHUNK:(c + 1) * SGU_CHUNK] * mixed)
    yc = jnp.concatenate(yc_parts, axis=0)

    rx, rg = part(9), part(10)
    rc = rc_ref[...]
    lcw = lcw_ref[...]
    xc = (lcw[3:4] * rx + lcw[2:3] * _shift_rows(rx, rc, 1) + lcw[1:2] * _shift_rows(rx, rc, 2)
          + lcw[0:1] * _shift_rows(rx, rc, 3) + lcb_ref[...])
    rc_ref[...] = rx[ts - 8:]
    xcb = xc.astype(BF)
    rgate = jax.nn.sigmoid(_dot(xcb, wa_ref[...]) + ba_ref[...])
    igate = jax.nn.sigmoid(_dot(xcb, wx_ref[...]) + bx_ref[...])
    log_a = (-LRU_C) * rgate * _softplus(-lam_ref[...])
    a = jnp.exp(log_a)
    t = jnp.tanh(log_a)
    mult = jnp.sqrt(-2.0 * t / (1.0 - t))
    bcur = mult * (igate * xc)
    acur = a
    s = 1
    while s < ts:
        a_s = _shift_rows_fill(acur, s, 1.0)
        b_s = _shift_rows_fill(bcur, s, 0.0)
        bcur = acur * b_s + bcur
        acur = acur * a_s
        s *= 2
    hseq = bcur + acur * hc_ref[7:8, :]
    hc_ref[...] = hseq[ts - 8:]
    yd = hseq * _gelu_tanh(rg)

    y_ref[0] = jnp.concatenate([ya, yb, yc, yd], axis=-1).astype(y_ref.dtype)


def _const_spec(shape):
    nd = len(shape)
    return pl.BlockSpec(shape, lambda *_: (0,) * nd)


def _mixer_call(x, g0, win, scw, wal, bal, gng, lng, lnb, sgw, sgb, lcw, lcb, wa, ba, wx, bx, lam):
    b, s, d = x.shape
    ts = SEQ_TILE
    consts = (g0, win, scw, wal, bal, gng, lng, lnb, sgw, sgb, lcw, lcb, wa, ba, wx, bx, lam)
    return pl.pallas_call(
        _mixer_kernel,
        out_shape=jax.ShapeDtypeStruct((b, s, D_MODEL), BF),
        grid=(b, s // ts),
        in_specs=[pl.BlockSpec((1, ts, d), lambda i, j: (i, j, 0))] + [_const_spec(c.shape) for c in consts],
        out_specs=pl.BlockSpec((1, ts, D_MODEL), lambda i, j: (i, j, 0)),
        scratch_shapes=[pltpu.VMEM((8, BRANCH_W), F32), pltpu.VMEM((8, BRANCH_W), F32),
                        pltpu.VMEM((BRANCH_W, BRANCH_W), F32), pltpu.VMEM((8, BRANCH_W), F32)],
        compiler_params=pltpu.CompilerParams(dimension_semantics=("arbitrary", "arbitrary"),
                                             vmem_limit_bytes=VMEM_LIMIT),
        name="mixer_seq",
    )(x, *consts)


def _merge_kernel(x_ref, y_ref, g0_ref, g1_ref, wg_ref, bg_ref, wb_ref, wo_ref, out_ref):
    x = x_ref[...]
    h = _rms(x, g0_ref[...]).astype(BF)
    y = y_ref[...]
    merged = jnp.zeros(x.shape, F32)
    for kk in range(4):
        gate = jax.nn.sigmoid(_dot(h, wg_ref[:, kk * D_MODEL:(kk + 1) * D_MODEL]) + bg_ref[kk:kk + 1, :])
        br = _dot(y[:, kk * BRANCH_W:(kk + 1) * BRANCH_W], wb_ref[kk])
        merged = merged + gate * br
    out = _dot(merged.astype(BF), wo_ref[...])
    out_ref[...] = x + _rms(out, g1_ref[...])


def _merge_call(x2, y2, g0, g1, wg, bg, wb, wo):
    t, d = x2.shape
    tm = ROW_TILE
    consts = (g0, g1, wg, bg, wb, wo)
    return pl.pallas_call(
        _merge_kernel,
        out_shape=jax.ShapeDtypeStruct((t, d), F32),
        grid=(t // tm,),
        in_specs=[pl.BlockSpec((tm, d), lambda i: (i, 0)), pl.BlockSpec((tm, d), lambda i: (i, 0))]
        + [_const_spec(c.shape) for c in consts],
        out_specs=pl.BlockSpec((tm, d), lambda i: (i, 0)),
        compiler_params=pltpu.CompilerParams(dimension_semantics=("parallel",), vmem_limit_bytes=VMEM_LIMIT),
        name="mixer_merge",
    )(x2, y2, *consts)


def _kv_kernel(mem_ref, g_ref, wkv_ref, kv_ref):
    mn = _rms(mem_ref[...], g_ref[...]).astype(BF)
    kv_ref[...] = _dot(mn, wkv_ref[...]).astype(kv_ref.dtype)


def _kv_call(mem2, g4, wkv):
    m, d = mem2.shape
    n = wkv.shape[1]
    tn = 512
    return pl.pallas_call(
        _kv_kernel,
        out_shape=jax.ShapeDtypeStruct((m, n), BF),
        grid=(n // tn,),
        in_specs=[_const_spec((m, d)), _const_spec((1, d)), pl.BlockSpec((d, tn), lambda j: (0, j))],
        out_specs=pl.BlockSpec((m, tn), lambda j: (0, j)),
        compiler_params=pltpu.CompilerParams(dimension_semantics=("parallel",), vmem_limit_bytes=VMEM_LIMIT),
        name="xattn_kv",
    )(mem2, g4, wkv)


def _xattn_kernel(x_ref, kv_ref, g2_ref, g3_ref, wq_ref, wo_ref, out_ref):
    x = x_ref[0]
    h = _rms(x, g2_ref[...]).astype(BF)
    q = (_dot(h, wq_ref[...]) * (XA_HEAD_DIM ** -0.5)).astype(BF)
    kv = kv_ref[0]
    outs = []
    for hd in range(XA_HEADS):
        lo, hi = hd * XA_HEAD_DIM, (hd + 1) * XA_HEAD_DIM
        sc = _dot_nt(q[:, lo:hi], kv[:, lo:hi])
        e = jnp.exp(sc - jnp.max(sc, axis=-1, keepdims=True))
        l = jnp.sum(e, axis=-1, keepdims=True)
        outs.append(_dot(e.astype(BF), kv[:, D_MODEL + lo:D_MODEL + hi]) / l)
    o = jnp.concatenate(outs, axis=-1).astype(BF)
    y = _dot(o, wo_ref[...])
    out_ref[0] = x + _rms(y, g3_ref[...])


def _xattn_call(x, kv, g2, g3, wq, wo):
    b, s, d = x.shape
    tm = ROW_TILE
    return pl.pallas_call(
        _xattn_kernel,
        out_shape=jax.ShapeDtypeStruct((b, s, d), F32),
        grid=(b, s // tm),
        in_specs=[pl.BlockSpec((1, tm, d), lambda i, j: (i, j, 0)),
                  pl.BlockSpec((1, N_MEM, 2 * d), lambda i, j: (i, 0, 0)),
                  _const_spec(g2.shape), _const_spec(g3.shape), _const_spec(wq.shape), _const_spec(wo.shape)],
        out_specs=pl.BlockSpec((1, tm, d), lambda i, j: (i, j, 0)),
        compiler_params=pltpu.CompilerParams(dimension_semantics=("parallel", "parallel"),
                                             vmem_limit_bytes=VMEM_LIMIT),
        name="xattn",
    )(x, kv, g2, g3, wq, wo)


def _ffn_kernel(x_ref, g5_ref, g6_ref, wup_ref, cw_ref, cb_ref, wdn_ref, out_ref, carry_ref):
    tm = ROW_TILE
    fc = FF_CHUNK

    @pl.when(pl.program_id(1) == 0)
    def _():
        carry_ref[...] = jnp.zeros_like(carry_ref)

    x = x_ref[0]
    h = _rms(x, g5_ref[...]).astype(BF)
    acc = jnp.zeros(x.shape, F32)

    def conv(u, col):
        tail = carry_ref[:, col:col + fc]
        cw = cw_ref[:, col:col + fc]
        y = (cw[2:3] * u + cw[1:2] * _shift_rows(u, tail, 1) + cw[0:1] * _shift_rows(u, tail, 2)
             + cb_ref[:, col:col + fc])
        carry_ref[:, col:col + fc] = u[tm - 8:]
        return y

    for c in range(D_FF // fc):
        cg, cv = c * fc, D_FF + c * fc
        ug = conv(_dot(h, wup_ref[:, cg:cg + fc]), cg)
        uv = conv(_dot(h, wup_ref[:, cv:cv + fc]), cv)
        act = (_gelu_tanh(ug) * uv).astype(BF)
        acc = acc + _dot(act, wdn_ref[cg:cg + fc, :])
    out_ref[0] = x + _rms(acc, g6_ref[...])


def _ffn_call(x, g5, g6, wup, cw, cb, wdn):
    b, s, d = x.shape
    tm = ROW_TILE
    consts = (g5, g6, wup, cw, cb, wdn)
    return pl.pallas_call(
        _ffn_kernel,
        out_shape=jax.ShapeDtypeStruct((b, s, d), F32),
        grid=(b, s // tm),
        in_specs=[pl.BlockSpec((1, tm, d), lambda i, j: (i, j, 0))] + [_const_spec(c.shape) for c in consts],
        out_specs=pl.BlockSpec((1, tm, d), lambda i, j: (i, j, 0)),
        scratch_shapes=[pltpu.VMEM((8, 2 * D_FF), F32)],
        compiler_params=pltpu.CompilerParams(dimension_semantics=("arbitrary", "arbitrary"),
                                             vmem_limit_bytes=VMEM_LIMIT),
        name="conv_ffn",
    )(x, *consts)


def _block_diag(wg):
    g, c, _ = wg.shape
    eye = jnp.eye(g, dtype=wg.dtype)
    return jnp.einsum('gio,gh->giho', wg, eye).reshape(g * c, g * c)


def kernel(x, mem, norm_g, w_in, sc_conv_w, gla_w_alpha, gla_b_alpha, gla_norm_g, sgu_ln_g, sgu_ln_b, sgu_w, sgu_b, lru_conv_w, lru_conv_b, lru_w_a, lru_b_a, lru_w_x, lru_b_x, lru_lambda, w_gate, b_gate, w_branch, w_mix_out, xa_wq, xa_wkv, xa_wo, ffn_w_up, ffn_conv_w, ffn_conv_b, ffn_w_down):
    bsz, s, d = x.shape
    depth = norm_g.shape[0]
    lr0 = 7 * BRANCH_W
    mem2 = mem.reshape(bsz * N_MEM, d)
    row = lambda a: a.reshape(1, -1)
    for l in range(depth):
        g = norm_g[l]
        wi = w_in[l]
        win = jnp.concatenate(
            [wi[:, :lr0], wi[:, lr0 + GLA_RANK:], wi[:, lr0:lr0 + GLA_RANK],
             jnp.zeros((d, LR_PAD - GLA_RANK), wi.dtype)], axis=1).astype(BF)
        wal = jnp.concatenate([gla_w_alpha[l], jnp.zeros((LR_PAD - GLA_RANK, BRANCH_W), F32)], axis=0).astype(BF)
        sgw = jnp.transpose(sgu_w[l], (1, 0, 2)).reshape(SGU_CHUNK, N_GROUPS * SGU_CHUNK).astype(BF)
        sgb = jnp.repeat(sgu_b[l].T, GROUP_W, axis=1)
        y = _mixer_call(
            x, row(g[0]), win, sc_conv_w[l], wal, row(gla_b_alpha[l]), row(gla_norm_g[l]),
            row(sgu_ln_g[l]), row(sgu_ln_b[l]), sgw, sgb, lru_conv_w[l], row(lru_conv_b[l]),
            _block_diag(lru_w_a[l]).astype(BF), row(lru_b_a[l]), _block_diag(lru_w_x[l]).astype(BF),
            row(lru_b_x[l]), row(lru_lambda[l]))
        x2 = _merge_call(x.reshape(bsz * s, d), y.reshape(bsz * s, d), row(g[0]), row(g[1]),
                         w_gate[l].astype(BF), b_gate[l], w_branch[l].astype(BF), w_mix_out[l].astype(BF))
        x = x2.reshape(bsz, s, d)
        kv = _kv_call(mem2, row(g[4]), xa_wkv[l].astype(BF)).reshape(bsz, N_MEM, 2 * d)
        x = _xattn_call(x, kv, row(g[2]), row(g[3]), xa_wq[l].astype(BF), xa_wo[l].astype(BF))
        x = _ffn_call(x, row(g[5]), row(g[6]), ffn_w_up[l].astype(BF), ffn_conv_w[l], row(ffn_conv_b[l]),
                      ffn_w_down[l].astype(BF))
    return x
```

```python
import functools

import jax
import jax.numpy as jnp
from jax import lax
from jax.experimental import pallas as pl
from jax.experimental.pallas import tpu as pltpu

BF = jnp.bfloat16
F32 = jnp.float32

D_MODEL = 1024
BRANCH_W = 256
N_GROUPS = 4
GROUP_W = 64
GLA_CHUNK = 64
GLA_RANK = 16
GLA_TAU = 16.0
SGU_CHUNK = 128
LRU_C = 8.0
N_MEM = 256
XA_HEADS = 4
XA_HEAD_DIM = D_MODEL // XA_HEADS
D_FF = 2816
EPS = 1e-6
N_MAIN = 11 * BRANCH_W
LR_PAD = 128
D_IN_PAD = N_MAIN + LR_PAD

SEQ_TILE = 256
ROW_TILE = 256
MERGE_COL_BLOCK = 256
MERGE_PIPE_DEPTH = 2
FFN_TILE = 256
FFN_PIPE_DEPTH = 2
FF_CHUNK = 256
VMEM_LIMIT = 56 * 1024 * 1024


def _dot(a, b):
    return jnp.dot(a, b, preferred_element_type=F32)


def _dot_nt(a, b):
    return lax.dot_general(a, b, (((1,), (1,)), ((), ())), preferred_element_type=F32)


def _rms(x, g):
    ms = jnp.mean(x * x, axis=-1, keepdims=True)
    return x * lax.rsqrt(ms + EPS) * g


def _gelu_tanh(x):
    return 0.5 * x * (1.0 + jnp.tanh(0.7978845608028654 * (x + 0.044715 * (x * x * x))))


def _softplus(x):
    return jnp.maximum(x, 0.0) + jnp.log1p(jnp.exp(-jnp.abs(x)))


def _shift_rows(z, tail8, k):
    zr = pltpu.roll(z, k, axis=0)
    cr = pltpu.roll(tail8, k, axis=0)
    row = lax.broadcasted_iota(jnp.int32, tail8.shape, 0)
    head = jnp.where(row < k, cr, zr[:8])
    return jnp.concatenate([head, zr[8:]], axis=0)


def _shift_rows_fill(z, k, fill):
    n, w = z.shape
    if k % 8 == 0:
        return jnp.concatenate([jnp.full((k, w), fill, z.dtype), z[: n - k]], axis=0)
    zr = pltpu.roll(z, k, axis=0)
    row = lax.broadcasted_iota(jnp.int32, (8, w), 0)
    head = jnp.where(row < k, jnp.full((8, w), fill, z.dtype), zr[:8])
    return jnp.concatenate([head, zr[8:]], axis=0)


def _hi_lo(x):
    hi = x.astype(BF)
    lo = (x - hi.astype(F32)).astype(BF)
    return hi, lo


def _mixer_kernel(x_ref, g0_ref, win_ref, scw_ref, wal_ref, bal_ref, gng_ref, lng_ref, lnb_ref,
                  sgw_ref, sgb_ref, lcw_ref, lcb_ref, wa_ref, ba_ref, wx_ref, bx_ref, lam_ref,
                  y_ref, zc_ref, rc_ref, st_ref, hc_ref):
    ts = SEQ_TILE
    w = BRANCH_W

    @pl.when(pl.program_id(1) == 0)
    def _():
        zc_ref[...] = jnp.zeros_like(zc_ref)
        rc_ref[...] = jnp.zeros_like(rc_ref)
        st_ref[...] = jnp.zeros_like(st_ref)
        hc_ref[...] = jnp.zeros_like(hc_ref)

    x = x_ref[0]
    h = _rms(x, g0_ref[...]).astype(BF)
    proj = _dot(h, win_ref[...])

    def part(i):
        return proj[:, w * i: w * (i + 1)]

    lane = lax.broadcasted_iota(jnp.int32, (1, w), 1)
    lane_grp = lane >> 6

    z = part(1) * part(2)
    zc = zc_ref[...]
    scw = scw_ref[...]
    conv = scw[2:3] * z + scw[1:2] * _shift_rows(z, zc, 1) + scw[0:1] * _shift_rows(z, zc, 2)
    ya = part(0) * conv
    zc_ref[...] = z[ts - 8:]

    q, k, v, r = part(3), part(4), part(5), part(6)
    alr = proj[:, N_MAIN:].astype(BF)
    logit = _dot(alr, wal_ref[...]) + bal_ref[...]
    glog = (jnp.minimum(logit, 0.0) - jnp.log1p(jnp.exp(-jnp.abs(logit)))) * (1.0 / GLA_TAU)
    ri = lax.broadcasted_iota(jnp.int32, (ts, ts), 0)
    ci = lax.broadcasted_iota(jnp.int32, (ts, ts), 1)
    same_chunk = (ri >> 6) == (ci >> 6)
    l_incl = jnp.where(same_chunk & (ci <= ri), 1.0, 0.0).astype(BF)
    u_strict = jnp.where(same_chunk & (ci > ri), 1.0, 0.0).astype(BF)
    g_hi, g_lo = _hi_lo(glog)
    gcum = _dot(l_incl, g_hi) + _dot(l_incl, g_lo)
    grev = _dot(u_strict, g_hi) + _dot(u_strict, g_lo)
    qd = q * (GROUP_W ** -0.5) * jnp.exp(gcum)
    kd = (k * jnp.exp(-gcum)).astype(BF)
    kte = (k * jnp.exp(grev)).astype(BF)
    vb = v.astype(BF)
    qd_b = qd.astype(BF)
    zero_b = jnp.zeros_like(qd_b)
    q_exp = jnp.concatenate([jnp.where(lane_grp == hh, qd_b, zero_b) for hh in range(N_GROUPS)], axis=0)
    s_exp = _dot_nt(q_exp, kd)
    r4 = lax.broadcasted_iota(jnp.int32, (N_GROUPS * ts, ts), 0) & (ts - 1)
    c4 = lax.broadcasted_iota(jnp.int32, (N_GROUPS * ts, ts), 1)
    s_exp = jnp.where(((r4 >> 6) == (c4 >> 6)) & (c4 <= r4), s_exp, 0.0)
    o_exp = _dot(s_exp.astype(BF), vb)
    o = jnp.zeros((ts, w), F32)
    for hh in range(N_GROUPS):
        o = o + jnp.where(lane_grp == hh, o_exp[hh * ts:(hh + 1) * ts], 0.0)
    vt = v.T.astype(BF)
    bd = (lax.broadcasted_iota(jnp.int32, (w, w), 0) >> 6) == (lax.broadcasted_iota(jnp.int32, (w, w), 1) >> 6)
    st = st_ref[...]
    n_chunks = ts // GLA_CHUNK
    o_inter = []
    zero_kte = jnp.zeros_like(kte)
    row_chunk = lax.broadcasted_iota(jnp.int32, (ts, w), 0) >> 6
    for c in range(n_chunks):
        lo_r, hi_r = c * GLA_CHUNK, (c + 1) * GLA_CHUNK
        o_inter.append(_dot_nt(qd_b[lo_r:hi_r], st.astype(BF)))
        kv_t = _dot(vt, jnp.where(row_chunk == c, kte, zero_kte))
        decay = jnp.exp(gcum[hi_r - 1:hi_r])
        st = st * decay + jnp.where(bd, kv_t, 0.0)
    st_ref[...] = st
    o = o + jnp.concatenate(o_inter, axis=0)
    bd_mean = jnp.where(bd, 1.0 / GROUP_W, 0.0).astype(BF)
    ms = _dot((o * o).astype(BF), bd_mean)
    o = o * lax.rsqrt(ms + EPS) * gng_ref[...]
    yb = o * (r * jax.nn.sigmoid(r))

    su, sv = part(7), part(8)
    mu = jnp.mean(sv, axis=-1, keepdims=True)
    svc = sv - mu
    var = jnp.mean(svc * svc, axis=-1, keepdims=True)
    vn = (svc * lax.rsqrt(var + EPS) * lng_ref[...] + lnb_ref[...]).astype(BF)
    wi = lax.broadcasted_iota(jnp.int32, (SGU_CHUNK, N_GROUPS * SGU_CHUNK), 0)
    wj = lax.broadcasted_iota(jnp.int32, (SGU_CHUNK, N_GROUPS * SGU_CHUNK), 1) & (SGU_CHUNK - 1)
    sgw = sgw_ref[...]
    w_mask = jnp.where(wj <= wi, sgw, jnp.zeros_like(sgw))
    zero_vn = jnp.zeros((SGU_CHUNK, w), BF)
    yc_parts = []
    for c in range(ts // SGU_CHUNK):
        vc = vn[c * SGU_CHUNK:(c + 1) * SGU_CHUNK]
        v_exp = jnp.concatenate([jnp.where(lane_grp == gg, vc, zero_vn) for gg in range(N_GROUPS)], axis=0)
        mixed = _dot(w_mask, v_exp) + sgb_ref[...]
        yc_parts.append(su[c * SGU_CHUNK:(c + 1) * SGU_CHUNK] * mixed)
    yc = jnp.concatenate(yc_parts, axis=0)

    rx, rg = part(9), part(10)
    rc = rc_ref[...]
    lcw = lcw_ref[...]
    xc = (lcw[3:4] * rx + lcw[2:3] * _shift_rows(rx, rc, 1) + lcw[1:2] * _shift_rows(rx, rc, 2)
          + lcw[0:1] * _shift_rows(rx, rc, 3) + lcb_ref[...])
    rc_ref[...] = rx[ts - 8:]
    xcb = xc.astype(BF)
    rgate = jax.nn.sigmoid(_dot(xcb, wa_ref[...]) + ba_ref[...])
    igate = jax.nn.sigmoid(_dot(xcb, wx_ref[...]) + bx_ref[...])
    log_a = (-LRU_C) * rgate * _softplus(-lam_ref[...])
    a = jnp.exp(log_a)
    t = jnp.tanh(log_a)
    mult = jnp.sqrt(-2.0 * t / (1.0 - t))
    bcur = mult * (igate * xc)
    acur = a
    s = 1
    while s < ts:
        a_s = _shift_rows_fill(acur, s, 1.0)
        b_s = _shift_rows_fill(bcur, s, 0.0)
        bcur = acur * b_s + bcur
        acur = acur * a_s
        s *= 2
    hseq = bcur + acur * hc_ref[7:8, :]
    hc_ref[...] = hseq[ts - 8:]
    yd = hseq * _gelu_tanh(rg)

    y_ref[0] = jnp.concatenate([ya, yb, yc, yd], axis=-1).astype(y_ref.dtype)


def _const_spec(shape):
    nd = len(shape)
    return pl.BlockSpec(shape, lambda *_: (0,) * nd, pipeline_mode=pl.Buffered(1))


def _mixer_call(x, g0, win, scw, wal, bal, gng, lng, lnb, sgw, sgb, lcw, lcb, wa, ba, wx, bx, lam):
    b, s, d = x.shape
    ts = SEQ_TILE
    consts = (g0, win, scw, wal, bal, gng, lng, lnb, sgw, sgb, lcw, lcb, wa, ba, wx, bx, lam)
    return pl.pallas_call(
        _mixer_kernel,
        out_shape=jax.ShapeDtypeStruct((b, s, D_MODEL), BF),
        grid=(b, s // ts),
        in_specs=[pl.BlockSpec((1, ts, d), lambda i, j: (i, j, 0))] + [_const_spec(c.shape) for c in consts],
        out_specs=pl.BlockSpec((1, ts, D_MODEL), lambda i, j: (i, j, 0)),
        scratch_shapes=[pltpu.VMEM((8, BRANCH_W), F32), pltpu.VMEM((8, BRANCH_W), F32),
                        pltpu.VMEM((BRANCH_W, BRANCH_W), F32), pltpu.VMEM((8, BRANCH_W), F32)],
        compiler_params=pltpu.CompilerParams(dimension_semantics=("arbitrary", "arbitrary"),
                                             vmem_limit_bytes=VMEM_LIMIT),
        name="mixer_seq",
    )(x, *consts)


def _merge_kernel(x_ref, y_ref, g0_ref, g1_ref, wg_ref, bg_ref, wb_ref, wo_ref, out_ref):
    x = x_ref[...]
    h = _rms(x, g0_ref[...]).astype(BF)
    y = y_ref[...]
    merged = jnp.zeros(x.shape, F32)
    for kk in range(4):
        zg = _dot(h, wg_ref[:, kk * D_MODEL:(kk + 1) * D_MODEL]) + bg_ref[kk:kk + 1, :]
        br = _dot(y[:, kk * BRANCH_W:(kk + 1) * BRANCH_W], wb_ref[kk])
        merged = merged + (jnp.tanh(zg) + 1.0) * br
    out = _dot(merged.astype(BF), wo_ref[...])
    out_ref[...] = x + _rms(out, g1_ref[...])


def _merge_call(x2, y2, g0, g1, wg, bg, wb, wo):
    t, d = x2.shape
    tm = ROW_TILE
    consts = (g0, g1, wg, bg, wb, wo)
    return pl.pallas_call(
        _merge_kernel,
        out_shape=jax.ShapeDtypeStruct((t, d), F32),
        grid=(t // tm,),
        in_specs=[pl.BlockSpec((tm, d), lambda i: (i, 0)), pl.BlockSpec((tm, d), lambda i: (i, 0))]
        + [_const_spec(c.shape) for c in consts],
        out_specs=pl.BlockSpec((tm, d), lambda i: (i, 0)),
        compiler_params=pltpu.CompilerParams(dimension_semantics=("parallel",), vmem_limit_bytes=VMEM_LIMIT),
        name="mixer_merge",
    )(x2, y2, *consts)


def _kv_kernel(mem_ref, g_ref, wkv_ref, kv_ref):
    mn = _rms(mem_ref[...], g_ref[...]).astype(BF)
    kv_ref[...] = _dot(mn, wkv_ref[...]).astype(kv_ref.dtype)


def _kv_call(mem2, g4, wkv):
    m, d = mem2.shape
    n = wkv.shape[1]
    tn = 512
    return pl.pallas_call(
        _kv_kernel,
        out_shape=jax.ShapeDtypeStruct((m, n), BF),
        grid=(n // tn,),
        in_specs=[_const_spec((m, d)), _const_spec((1, d)), pl.BlockSpec((d, tn), lambda j: (0, j))],
        out_specs=pl.BlockSpec((m, tn), lambda j: (0, j)),
        compiler_params=pltpu.CompilerParams(dimension_semantics=("parallel",), vmem_limit_bytes=VMEM_LIMIT),
        name="xattn_kv",
    )(mem2, g4, wkv)


def _xattn_kernel(x_ref, kv_ref, g2_ref, g3_ref, wq_ref, wo_ref, out_ref):
    x = x_ref[0]
    h = _rms(x, g2_ref[...]).astype(BF)
    q = (_dot(h, wq_ref[...]) * (XA_HEAD_DIM ** -0.5)).astype(BF)
    kv = kv_ref[0]
    outs = []
    for hd in range(XA_HEADS):
        lo, hi = hd * XA_HEAD_DIM, (hd + 1) * XA_HEAD_DIM
        sc = _dot_nt(q[:, lo:hi], kv[:, lo:hi])
        e = jnp.exp(sc - jnp.max(sc, axis=-1, keepdims=True))
        l = jnp.sum(e, axis=-1, keepdims=True)
        outs.append(_dot(e.astype(BF), kv[:, D_MODEL + lo:D_MODEL + hi]) / l)
    o = jnp.concatenate(outs, axis=-1).astype(BF)
    y = _dot(o, wo_ref[...])
    out_ref[0] = x + _rms(y, g3_ref[...])


def _xattn_call(x, kv, g2, g3, wq, wo):
    b, s, d = x.shape
    tm = ROW_TILE
    return pl.pallas_call(
        _xattn_kernel,
        out_shape=jax.ShapeDtypeStruct((b, s, d), F32),
        grid=(b, s // tm),
        in_specs=[pl.BlockSpec((1, tm, d), lambda i, j: (i, j, 0)),
                  pl.BlockSpec((1, N_MEM, 2 * d), lambda i, j: (i, 0, 0)),
                  _const_spec(g2.shape), _const_spec(g3.shape), _const_spec(wq.shape), _const_spec(wo.shape)],
        out_specs=pl.BlockSpec((1, tm, d), lambda i, j: (i, j, 0)),
        compiler_params=pltpu.CompilerParams(dimension_semantics=("parallel", "parallel"),
                                             vmem_limit_bytes=VMEM_LIMIT),
        name="xattn",
    )(x, kv, g2, g3, wq, wo)


def _ffn_kernel(x_ref, g5_ref, g6_ref, wup_ref, cw_ref, cb_ref, wdn_ref, out_ref, carry_ref):
    tm = FFN_TILE
    fc = FF_CHUNK

    @pl.when(pl.program_id(1) == 0)
    def _():
        carry_ref[...] = jnp.zeros_like(carry_ref)

    x = x_ref[0]
    h = _rms(x, g5_ref[...]).astype(BF)
    acc = jnp.zeros(x.shape, F32)

    def conv(u, col):
        tail = carry_ref[:, col:col + fc]
        cw = cw_ref[:, col:col + fc]
        y = (cw[2:3] * u + cw[1:2] * _shift_rows(u, tail, 1) + cw[0:1] * _shift_rows(u, tail, 2)
             + cb_ref[:, col:col + fc])
        carry_ref[:, col:col + fc] = u[tm - 8:]
        return y

    def up_g(c):
        return _dot(h, wup_ref[:, c * fc:(c + 1) * fc])

    def up_v(c):
        return _dot(h, wup_ref[:, D_FF + c * fc:D_FF + (c + 1) * fc])

    n_chunks = D_FF // fc
    depth = FFN_PIPE_DEPTH
    ups = [(up_g(c), up_v(c)) for c in range(depth)]
    for c in range(n_chunks):
        if c + depth < n_chunks:
            ups.append((up_g(c + depth), up_v(c + depth)))
        ug, uv = ups[c]
        yg = conv(ug, c * fc)
        inner = yg * (yg * yg * (0.7978845608028654 * 0.044715) + 0.7978845608028654)
        act = (yg * (jnp.tanh(inner) + 1.0) * conv(uv, D_FF + c * fc)).astype(BF)
        acc = acc + _dot(act, wdn_ref[c * fc:(c + 1) * fc, :])
    out_ref[0] = x + _rms(acc, g6_ref[...])


def _ffn_call(x, g5, g6, wup, cw, cb, wdn):
    b, s, d = x.shape
    tm = FFN_TILE
    consts = (g5, g6, wup, cw, cb, wdn)
    return pl.pallas_call(
        _ffn_kernel,
        out_shape=jax.ShapeDtypeStruct((b, s, d), F32),
        grid=(b, s // tm),
        in_specs=[pl.BlockSpec((1, tm, d), lambda i, j: (i, j, 0))] + [_const_spec(c.shape) for c in consts],
        out_specs=pl.BlockSpec((1, tm, d), lambda i, j: (i, j, 0)),
        scratch_shapes=[pltpu.VMEM((8, 2 * D_FF), F32)],
        compiler_params=pltpu.CompilerParams(dimension_semantics=("arbitrary", "arbitrary"),
                                             vmem_limit_bytes=VMEM_LIMIT),
        name="conv_ffn",
    )(x, *consts)


def _block_diag(wg):
    g, c, _ = wg.shape
    eye = jnp.eye(g, dtype=wg.dtype)
    return jnp.einsum('gio,gh->giho', wg, eye).reshape(g * c, g * c)


def kernel(x, mem, norm_g, w_in, sc_conv_w, gla_w_alpha, gla_b_alpha, gla_norm_g, sgu_ln_g, sgu_ln_b, sgu_w, sgu_b, lru_conv_w, lru_conv_b, lru_w_a, lru_b_a, lru_w_x, lru_b_x, lru_lambda, w_gate, b_gate, w_branch, w_mix_out, xa_wq, xa_wkv, xa_wo, ffn_w_up, ffn_conv_w, ffn_conv_b, ffn_w_down):
    bsz, s, d = x.shape
    depth = norm_g.shape[0]
    lr0 = 7 * BRANCH_W
    mem2 = mem.reshape(bsz * N_MEM, d)
    row = lambda a: a.reshape(1, -1)
    for l in range(depth):
        g = norm_g[l]
        wi = w_in[l]
        win = jnp.concatenate(
            [wi[:, :lr0], wi[:, lr0 + GLA_RANK:], wi[:, lr0:lr0 + GLA_RANK],
             jnp.zeros((d, LR_PAD - GLA_RANK), wi.dtype)], axis=1).astype(BF)
        wal = jnp.concatenate([gla_w_alpha[l], jnp.zeros((LR_PAD - GLA_RANK, BRANCH_W), F32)], axis=0).astype(BF)
        sgw = jnp.transpose(sgu_w[l], (1, 0, 2)).reshape(SGU_CHUNK, N_GROUPS * SGU_CHUNK).astype(BF)
        sgb = jnp.repeat(sgu_b[l].T, GROUP_W, axis=1)
        y = _mixer_call(
            x, row(g[0]), win, sc_conv_w[l], wal, row(gla_b_alpha[l]), row(gla_norm_g[l]),
            row(sgu_ln_g[l]), row(sgu_ln_b[l]), sgw, sgb, lru_conv_w[l], row(lru_conv_b[l]),
            _block_diag(lru_w_a[l]).astype(BF), row(lru_b_a[l]), _block_diag(lru_w_x[l]).astype(BF),
            row(lru_b_x[l]), row(lru_lambda[l]))
        x2 = _merge_call(x.reshape(bsz * s, d), y.reshape(bsz * s, d), row(g[0]), row(g[1]),
                         (0.5 * w_gate[l]).astype(BF), 0.5 * b_gate[l], (0.5 * w_branch[l]).astype(BF),
                         w_mix_out[l].astype(BF))
        x = x2.reshape(bsz, s, d)
        kv = _kv_call(mem2, row(g[4]), xa_wkv[l].astype(BF)).reshape(bsz, N_MEM, 2 * d)
        x = _xattn_call(x, kv, row(g[2]), row(g[3]), xa_wq[l].astype(BF), xa_wo[l].astype(BF))
        x = _ffn_call(x, row(g[5]), row(g[6]), ffn_w_up[l].astype(BF), ffn_conv_w[l], row(ffn_conv_b[l]),
                      (0.5 * ffn_w_down[l]).astype(BF))
    return x
```

```python
import jax
import jax.numpy as jnp
import numpy as np
from jax import lax
from jax.experimental import pallas as pl
from jax.experimental.pallas import tpu as pltpu

BF = jnp.bfloat16
F32 = jnp.float32

D_MODEL = 1024
BRANCH_W = 256
N_GROUPS = 4
GROUP_W = 64
GLA_CHUNK = 64
GLA_RANK = 16
GLA_TAU = 16.0
SGU_CHUNK = 128
LRU_C = 8.0
N_MEM = 256
XA_HEADS = 4
XA_HEAD_DIM = D_MODEL // XA_HEADS
D_FF = 2816
EPS = 1e-6
LR_PAD = 128

SEQ_TILE = 256
GATE_PIECE = 256
ROW_TILE = 256
FFN_TILE = 256
FFN_PIPE_DEPTH = 2
FF_CHUNK = 256
CAST_ROWS = 256
VMEM_LIMIT = 56 * 1024 * 1024


def _dot(a, b):
    return jnp.dot(a, b, preferred_element_type=F32)


def _dot_nt(a, b):
    return lax.dot_general(a, b, (((1,), (1,)), ((), ())), preferred_element_type=F32)


def _rms(x, g):
    ms = jnp.mean(x * x, axis=-1, keepdims=True)
    return x * lax.rsqrt(ms + EPS) * g


def _gelu_tanh(x):
    return 0.5 * x * (1.0 + jnp.tanh(0.7978845608028654 * (x + 0.044715 * (x * x * x))))


def _softplus(x):
    return jnp.maximum(x, 0.0) + jnp.log1p(jnp.exp(-jnp.abs(x)))


def _shift_rows(z, tail8, k):
    zr = pltpu.roll(z, k, axis=0)
    cr = pltpu.roll(tail8, k, axis=0)
    row = lax.broadcasted_iota(jnp.int32, tail8.shape, 0)
    head = jnp.where(row < k, cr, zr[:8])
    return jnp.concatenate([head, zr[8:]], axis=0)


def _shift_rows_fill(z, k, fill):
    n, w = z.shape
    if k % 8 == 0:
        return jnp.concatenate([jnp.full((k, w), fill, z.dtype), z[: n - k]], axis=0)
    zr = pltpu.roll(z, k, axis=0)
    row = lax.broadcasted_iota(jnp.int32, (8, w), 0)
    head = jnp.where(row < k, jnp.full((8, w), fill, z.dtype), zr[:8])
    return jnp.concatenate([head, zr[8:]], axis=0)


def _hi_lo(x):
    hi = x.astype(BF)
    lo = (x - hi.astype(F32)).astype(BF)
    return hi, lo


def _mixer_masks():
    ts, w = SEQ_TILE, BRANCH_W
    r = np.arange(ts)[:, None]
    c = np.arange(ts)[None, :]
    same_chunk = (r // GLA_CHUNK) == (c // GLA_CHUNK)
    l_incl = (same_chunk & (c <= r)).astype(np.float32)
    s_mask = np.tile(l_incl, (N_GROUPS, 1))
    g = np.arange(w)
    bd = ((g[:, None] // GROUP_W) == (g[None, :] // GROUP_W)).astype(np.float32)
    i = np.arange(SGU_CHUNK)[:, None]
    j = np.arange(N_GROUPS * SGU_CHUNK)[None, :] % SGU_CHUNK
    w_mask = (j <= i).astype(np.float32)
    return (jnp.asarray(l_incl, BF), jnp.asarray(s_mask, BF), jnp.asarray(bd, F32),
            jnp.asarray(bd / GROUP_W, BF), jnp.asarray(w_mask, BF))


def _mixer_kernel(x_ref, g0_ref, g1_ref, wab_ref, wcd_ref, wlr_ref, scw_ref, wal_ref, bal_ref, gng_ref,
                  lng_ref, lnb_ref, sgw_ref, sgb_ref, lcw_ref, lcb_ref, wax_ref, bax_ref, lam_ref,
                  lincl_ref, smask_ref, bd_ref, bdmean_ref, wmask_ref,
                  wg_ref, bg_ref, wb_ref, wo_ref,
                  out_ref, zc_ref, rc_ref, st_ref, hc_ref):
    ts = SEQ_TILE
    w = BRANCH_W

    @pl.when(pl.program_id(1) == 0)
    def _():
        zc_ref[...] = jnp.zeros_like(zc_ref)
        rc_ref[...] = jnp.zeros_like(rc_ref)
        st_ref[...] = jnp.zeros_like(st_ref)
        hc_ref[...] = jnp.zeros_like(hc_ref)

    x = x_ref[0]
    h = _rms(x, g0_ref[...]).astype(BF)
    lane_grp = lax.broadcasted_iota(jnp.int32, (1, w), 1) >> 6
    n_chunks = ts // GLA_CHUNK
    gp = GATE_PIECE

    def proj(w_ref, c0, c1):
        return _dot(h, w_ref[:, c0:c1])

    def gate_piece(kk, nb):
        c0 = kk * D_MODEL + nb * gp
        return _dot(h, wg_ref[:, c0:c0 + gp])

    def gate_pre(kk):
        return [gate_piece(kk, nb) for nb in range(D_MODEL // gp)]

    def branch_out(kk, yk):
        return _dot(yk.astype(BF), wb_ref[kk])

    def gated(kk, zg_pieces, br):
        zg = jnp.concatenate(zg_pieces, axis=-1)
        return (jnp.tanh(zg + bg_ref[kk:kk + 1, :]) + 1.0) * br

    p_d = proj(wcd_ref, 2 * w, 4 * w)
    p_a = proj(wab_ref, 0, 3 * w)

    rx, rg = p_d[:, 0:w], p_d[:, w:2 * w]
    rc = rc_ref[...]
    lcw = lcw_ref[...]
    xc = (lcw[3:4] * rx + lcw[2:3] * _shift_rows(rx, rc, 1) + lcw[1:2] * _shift_rows(rx, rc, 2)
          + lcw[0:1] * _shift_rows(rx, rc, 3) + lcb_ref[...])
    rc_ref[...] = rx[ts - 8:]
    gates = jax.nn.sigmoid(_dot(xc.astype(BF), wax_ref[...]) + bax_ref[...])
    p_b = proj(wab_ref, 3 * w, 7 * w)
    p_lr = proj(wlr_ref, 0, LR_PAD)
    p_c = proj(wcd_ref, 0, 2 * w)
    log_a = (-LRU_C) * gates[:, 0:w] * _softplus(-lam_ref[...])
    acur = jnp.exp(log_a)
    t = jnp.tanh(log_a)
    bcur = jnp.sqrt(-2.0 * t / (1.0 - t)) * (gates[:, w:2 * w] * xc)
    s = 1
    while s < ts:
        a_s = _shift_rows_fill(acur, s, 1.0)
        b_s = _shift_rows_fill(bcur, s, 0.0)
        bcur = acur * b_s + bcur
        acur = acur * a_s
        s *= 2
    hseq = bcur + acur * hc_ref[7:8, :]
    hc_ref[...] = hseq[ts - 8:]
    br_d = branch_out(3, hseq * _gelu_tanh(rg))
    zg_d = gate_pre(3)

    z = p_a[:, w:2 * w] * p_a[:, 2 * w:3 * w]
    zc = zc_ref[...]
    scw = scw_ref[...]
    conv = scw[2:3] * z + scw[1:2] * _shift_rows(z, zc, 1) + scw[0:1] * _shift_rows(z, zc, 2)
    zc_ref[...] = z[ts - 8:]
    br_a = branch_out(0, p_a[:, 0:w] * conv)
    merged = gated(3, zg_d, br_d)

    q, k, v, r = p_b[:, 0:w], p_b[:, w:2 * w], p_b[:, 2 * w:3 * w], p_b[:, 3 * w:4 * w]
    logit = _dot(p_lr.astype(BF), wal_ref[...]) + bal_ref[...]
    glog = (jnp.minimum(logit, 0.0) - jnp.log1p(jnp.exp(-jnp.abs(logit)))) * (1.0 / GLA_TAU)
    g_hi, g_lo = _hi_lo(glog)
    zg_a = [gate_piece(0, 0)]
    l_incl = lincl_ref[...]
    gcum = _dot(l_incl, g_hi) + _dot(l_incl, g_lo)
    g_last = [gcum[(c + 1) * GLA_CHUNK - 1:(c + 1) * GLA_CHUNK] for c in range(n_chunks)]
    g_last_rows = jnp.concatenate([jnp.broadcast_to(gl, (GLA_CHUNK, w)) for gl in g_last], axis=0)
    qd_b = (q * (GROUP_W ** -0.5) * jnp.exp(gcum)).astype(BF)
    kd = (k * jnp.exp(-gcum)).astype(BF)
    kte = (k * jnp.exp(g_last_rows - gcum)).astype(BF)
    vb = v.astype(BF)
    vt = v.T.astype(BF)
    zero_b = jnp.zeros_like(qd_b)
    q_exp = jnp.concatenate([jnp.where(lane_grp == hh, qd_b, zero_b) for hh in range(N_GROUPS)], axis=0)
    zg_a += [gate_piece(0, 1), gate_piece(0, 2)]
    s_exp = _dot_nt(q_exp, kd).astype(BF)
    zero_rows = jnp.zeros((GLA_CHUNK, w), BF)
    kv_t = []
    for c in range(n_chunks):
        kte_c = jnp.concatenate([kte[c * GLA_CHUNK:(c + 1) * GLA_CHUNK] if cc == c else zero_rows
                                 for cc in range(n_chunks)], axis=0)
        kv_t.append(_dot(vt, kte_c))
    zg_a.append(gate_piece(0, 3))
    s_exp = jnp.where(smask_ref[...] != 0, s_exp, jnp.zeros_like(s_exp))
    o_exp = _dot(s_exp, vb)
    bd = bd_ref[...]
    st = st_ref[...]
    o_inter = []
    for c in range(n_chunks):
        o_inter.append(_dot_nt(qd_b[c * GLA_CHUNK:(c + 1) * GLA_CHUNK], st.astype(BF)))
        st = st * jnp.exp(g_last[c]) + bd * kv_t[c]
    st_ref[...] = st
    zg_b = [gate_piece(1, 0)]
    o = jnp.concatenate(o_inter, axis=0)
    for hh in range(N_GROUPS):
        o = o + jnp.where(lane_grp == hh, o_exp[hh * ts:(hh + 1) * ts], 0.0)
    ms = _dot((o * o).astype(BF), bdmean_ref[...])
    zg_b.append(gate_piece(1, 1))
    o = o * lax.rsqrt(ms + EPS) * gng_ref[...]
    br_b = branch_out(1, o * (r * jax.nn.sigmoid(r)))
    merged = merged + gated(0, zg_a, br_a)
    zg_b.append(gate_piece(1, 2))

    su, sv = p_c[:, 0:w], p_c[:, w:2 * w]
    mu = jnp.mean(sv, axis=-1, keepdims=True)
    svc = sv - mu
    var = jnp.mean(svc * svc, axis=-1, keepdims=True)
    vn = (svc * lax.rsqrt(var + EPS) * lng_ref[...] + lnb_ref[...]).astype(BF)
    w_mask = sgw_ref[...] * wmask_ref[...]
    zero_vn = jnp.zeros((SGU_CHUNK, w), BF)
    parts = []
    for c in range(ts // SGU_CHUNK):
        vc = vn[c * SGU_CHUNK:(c + 1) * SGU_CHUNK]
        v_exp = jnp.concatenate([jnp.where(lane_grp == gg, vc, zero_vn) for gg in range(N_GROUPS)], axis=0)
        mixed = _dot(w_mask, v_exp) + sgb_ref[...]
        parts.append(su[c * SGU_CHUNK:(c + 1) * SGU_CHUNK] * mixed)
    zg_b.append(gate_piece(1, 3))
    br_c = branch_out(2, jnp.concatenate(parts, axis=0))
    merged = merged + gated(1, zg_b, br_b)

    out = jnp.zeros(x.shape, F32)
    n_pieces = D_MODEL // gp
    zg_c = [gate_piece(2, 0)]
    for nb in range(n_pieces):
        if nb + 1 < n_pieces:
            zg_c.append(gate_piece(2, nb + 1))
        cols = slice(nb * gp, (nb + 1) * gp)
        m_nb = merged[:, cols] + (jnp.tanh(zg_c[nb] + bg_ref[2:3, cols]) + 1.0) * br_c[:, cols]
        out = out + _dot(m_nb.astype(BF), wo_ref[cols, :])
    out_ref[0] = x + _rms(out, g1_ref[...])


def _const_spec(shape):
    nd = len(shape)
    return pl.BlockSpec(shape, lambda *_: (0,) * nd, pipeline_mode=pl.Buffered(1))


def _layer_spec(stacked, layer):
    shape = stacked.shape[1:]
    nd = len(shape)
    return pl.BlockSpec((None,) + shape, lambda *_: (layer,) + (0,) * nd, pipeline_mode=pl.Buffered(1))


def _specs(operands, layer):
    arrays, specs = [], []
    for op in operands:
        if isinstance(op, tuple):
            arrays.append(op[1])
            specs.append(_layer_spec(op[1], layer))
        else:
            arrays.append(op)
            specs.append(_const_spec(op.shape))
    return arrays, specs


def _mixer_call(x, layer, operands):
    b, s, d = x.shape
    ts = SEQ_TILE
    consts, const_specs = _specs(operands, layer)
    return pl.pallas_call(
        _mixer_kernel,
        out_shape=jax.ShapeDtypeStruct((b, s, d), F32),
        grid=(b, s // ts),
        in_specs=[pl.BlockSpec((1, ts, d), lambda i, j: (i, j, 0))] + const_specs,
        out_specs=pl.BlockSpec((1, ts, d), lambda i, j: (i, j, 0)),
        scratch_shapes=[pltpu.VMEM((8, BRANCH_W), F32), pltpu.VMEM((8, BRANCH_W), F32),
                        pltpu.VMEM((BRANCH_W, BRANCH_W), F32), pltpu.VMEM((8, BRANCH_W), F32)],
        compiler_params=pltpu.CompilerParams(dimension_semantics=("arbitrary", "arbitrary"),
                                             vmem_limit_bytes=VMEM_LIMIT),
        name="mixer",
    )(x, *consts)


def _kv_kernel(mem_ref, g_ref, wkv_ref, kv_ref):
    mn = _rms(mem_ref[...], g_ref[...]).astype(BF)
    kv_ref[...] = _dot(mn, wkv_ref[...]).astype(kv_ref.dtype)


def _kv_call(mem2, g4, wkv, layer):
    m, d = mem2.shape
    n = wkv.shape[-1]
    tn = 512
    return pl.pallas_call(
        _kv_kernel,
        out_shape=jax.ShapeDtypeStruct((m, n), BF),
        grid=(n // tn,),
        in_specs=[_const_spec((m, d)), _const_spec((1, d)),
                  pl.BlockSpec((None, d, tn), lambda j: (layer, 0, j))],
        out_specs=pl.BlockSpec((m, tn), lambda j: (0, j)),
        compiler_params=pltpu.CompilerParams(dimension_semantics=("parallel",), vmem_limit_bytes=VMEM_LIMIT),
        name="xattn_kv",
    )(mem2, g4, wkv)


def _xattn_kernel(x_ref, kv_ref, g2_ref, g3_ref, wq_ref, wo_ref, out_ref):
    x = x_ref[0]
    h = _rms(x, g2_ref[...]).astype(BF)
    q = (_dot(h, wq_ref[...]) * (XA_HEAD_DIM ** -0.5)).astype(BF)
    kv = kv_ref[0]
    outs = []
    for hd in range(XA_HEADS):
        lo, hi = hd * XA_HEAD_DIM, (hd + 1) * XA_HEAD_DIM
        sc = _dot_nt(q[:, lo:hi], kv[:, lo:hi])
        e = jnp.exp(sc - jnp.max(sc, axis=-1, keepdims=True))
        l = jnp.sum(e, axis=-1, keepdims=True)
        outs.append(_dot(e.astype(BF), kv[:, D_MODEL + lo:D_MODEL + hi]) / l)
    o = jnp.concatenate(outs, axis=-1).astype(BF)
    y = _dot(o, wo_ref[...])
    out_ref[0] = x + _rms(y, g3_ref[...])


def _xattn_call(x, kv, g2, g3, wq, wo, layer):
    b, s, d = x.shape
    tm = ROW_TILE
    return pl.pallas_call(
        _xattn_kernel,
        out_shape=jax.ShapeDtypeStruct((b, s, d), F32),
        grid=(b, s // tm),
        in_specs=[pl.BlockSpec((1, tm, d), lambda i, j: (i, j, 0)),
                  pl.BlockSpec((1, N_MEM, 2 * d), lambda i, j: (i, 0, 0)),
                  _const_spec(g2.shape), _const_spec(g3.shape), _layer_spec(wq, layer), _layer_spec(wo, layer)],
        out_specs=pl.BlockSpec((1, tm, d), lambda i, j: (i, j, 0)),
        compiler_params=pltpu.CompilerParams(dimension_semantics=("parallel", "parallel"),
                                             vmem_limit_bytes=VMEM_LIMIT),
        name="xattn",
    )(x, kv, g2, g3, wq, wo)


def _ffn_kernel(x_ref, g5_ref, g6_ref, wup_ref, cw_ref, cb_ref, wdn_ref, out_ref, carry_ref):
    tm = FFN_TILE
    fc = FF_CHUNK

    @pl.when(pl.program_id(1) == 0)
    def _():
        carry_ref[...] = jnp.zeros_like(carry_ref)

    x = x_ref[0]
    h = _rms(x, g5_ref[...]).astype(BF)
    acc = jnp.zeros(x.shape, F32)

    def conv(u, col):
        tail = carry_ref[:, col:col + fc]
        cw = cw_ref[:, col:col + fc]
        y = (cw[2:3] * u + cw[1:2] * _shift_rows(u, tail, 1) + cw[0:1] * _shift_rows(u, tail, 2)
             + cb_ref[:, col:col + fc])
        carry_ref[:, col:col + fc] = u[tm - 8:]
        return y

    def up_g(c):
        return _dot(h, wup_ref[:, c * fc:(c + 1) * fc])

    def up_v(c):
        return _dot(h, wup_ref[:, D_FF + c * fc:D_FF + (c + 1) * fc])

    n_chunks = D_FF // fc
    depth = FFN_PIPE_DEPTH
    ups = [(up_g(c), up_v(c)) for c in range(depth)]
    for c in range(n_chunks):
        if c + depth < n_chunks:
            ups.append((up_g(c + depth), up_v(c + depth)))
        ug, uv = ups[c]
        yg = conv(ug, c * fc)
        inner = yg * (yg * yg * (0.7978845608028654 * 0.044715) + 0.7978845608028654)
        act = (yg * (jnp.tanh(inner) + 1.0) * conv(uv, D_FF + c * fc)).astype(BF)
        acc = acc + _dot(act, wdn_ref[c * fc:(c + 1) * fc, :])
    out_ref[0] = x + _rms(acc, g6_ref[...])


def _ffn_call(x, layer, operands):
    b, s, d = x.shape
    tm = FFN_TILE
    consts, const_specs = _specs(operands, layer)
    return pl.pallas_call(
        _ffn_kernel,
        out_shape=jax.ShapeDtypeStruct((b, s, d), F32),
        grid=(b, s // tm),
        in_specs=[pl.BlockSpec((1, tm, d), lambda i, j: (i, j, 0))] + const_specs,
        out_specs=pl.BlockSpec((1, tm, d), lambda i, j: (i, j, 0)),
        scratch_shapes=[pltpu.VMEM((8, 2 * D_FF), F32)],
        compiler_params=pltpu.CompilerParams(dimension_semantics=("arbitrary", "arbitrary"),
                                             vmem_limit_bytes=VMEM_LIMIT),
        name="conv_ffn",
    )(x, *consts)


def _cast_kernel(w_ref, o_ref, *, scale):
    w = w_ref[...]
    if scale != 1.0:
        w = w * scale
    o_ref[...] = w.astype(o_ref.dtype)


def _cast_bf16(w, scale=1.0, cols=None):
    nl, r, c = w.shape
    cols = c if cols is None else cols
    tr = CAST_ROWS
    return pl.pallas_call(
        lambda w_ref, o_ref: _cast_kernel(w_ref, o_ref, scale=scale),
        out_shape=jax.ShapeDtypeStruct((nl, r, cols), BF),
        grid=(nl, r // tr),
        in_specs=[pl.BlockSpec((1, tr, cols), lambda l, i: (l, i, 0))],
        out_specs=pl.BlockSpec((1, tr, cols), lambda l, i: (l, i, 0)),
        compiler_params=pltpu.CompilerParams(dimension_semantics=("parallel", "parallel"),
                                             vmem_limit_bytes=VMEM_LIMIT),
        name="cast_bf16",
    )(w)


def _block_diag(wg):
    g, c, _ = wg.shape
    eye = jnp.eye(g, dtype=wg.dtype)
    return jnp.einsum('gio,gh->giho', wg, eye).reshape(g * c, g * c)


def kernel(x, mem, norm_g, w_in, sc_conv_w, gla_w_alpha, gla_b_alpha, gla_norm_g, sgu_ln_g, sgu_ln_b, sgu_w, sgu_b, lru_conv_w, lru_conv_b, lru_w_a, lru_b_a, lru_w_x, lru_b_x, lru_lambda, w_gate, b_gate, w_branch, w_mix_out, xa_wq, xa_wkv, xa_wo, ffn_w_up, ffn_conv_w, ffn_conv_b, ffn_w_down):
    bsz, s, d = x.shape
    depth = norm_g.shape[0]
    lr0 = 7 * BRANCH_W
    mem2 = mem.reshape(bsz * N_MEM, d)
    row = lambda a: a.reshape(1, -1)
    masks = _mixer_masks()
    w_ab = _cast_bf16(w_in, cols=lr0)
    w_cd = w_in[:, :, lr0 + GLA_RANK:].astype(BF)
    w_lr = jnp.pad(w_in[:, :, lr0:lr0 + GLA_RANK], ((0, 0), (0, 0), (0, LR_PAD - GLA_RANK))).astype(BF)
    w_g = _cast_bf16(w_gate, 0.5)
    w_b = _cast_bf16(w_branch.reshape(depth, N_GROUPS * BRANCH_W, d), 0.5).reshape(depth, 4, BRANCH_W, d)
    w_o = _cast_bf16(w_mix_out)
    w_q = _cast_bf16(xa_wq)
    w_kv = _cast_bf16(xa_wkv)
    w_xo = _cast_bf16(xa_wo)
    w_up = _cast_bf16(ffn_w_up)
    w_dn = _cast_bf16(ffn_w_down, 0.5)
    for l in range(depth):
        g = norm_g[l]
        wal = jnp.concatenate([gla_w_alpha[l], jnp.zeros((LR_PAD - GLA_RANK, BRANCH_W), F32)], axis=0).astype(BF)
        sgw = jnp.transpose(sgu_w[l], (1, 0, 2)).reshape(SGU_CHUNK, N_GROUPS * SGU_CHUNK).astype(BF)
        sgb = jnp.repeat(sgu_b[l].T, GROUP_W, axis=1)
        wax = jnp.concatenate([_block_diag(lru_w_a[l]), _block_diag(lru_w_x[l])], axis=1).astype(BF)
        bax = jnp.concatenate([lru_b_a[l], lru_b_x[l]]).reshape(1, -1)
        x = _mixer_call(x, l, (
            row(g[0]), row(g[1]), ("layer", w_ab), ("layer", w_cd), ("layer", w_lr), sc_conv_w[l], wal,
            row(gla_b_alpha[l]), row(gla_norm_g[l]), row(sgu_ln_g[l]), row(sgu_ln_b[l]), sgw, sgb,
            lru_conv_w[l], row(lru_conv_b[l]), wax, bax, row(lru_lambda[l]), *masks,
            ("layer", w_g), 0.5 * b_gate[l], ("layer", w_b), ("layer", w_o)))
        kv = _kv_call(mem2, row(g[4]), w_kv, l).reshape(bsz, N_MEM, 2 * d)
        x = _xattn_call(x, kv, row(g[2]), row(g[3]), w_q, w_xo, l)
        x = _ffn_call(x, l, (row(g[5]), row(g[6]), ("layer", w_up), ffn_conv_w[l], row(ffn_conv_b[l]),
                             ("layer", w_dn)))
    return x
```

```python
import jax
import jax.numpy as jnp
import numpy as np
from jax import lax
from jax.experimental import pallas as pl
from jax.experimental.pallas import tpu as pltpu

BF = jnp.bfloat16
F32 = jnp.float32

D_MODEL = 1024
BRANCH_W = 256
N_GROUPS = 4
GROUP_W = 64
GLA_CHUNK = 64
GLA_RANK = 16
GLA_TAU = 16.0
SGU_CHUNK = 128
LRU_C = 8.0
N_MEM = 256
XA_HEADS = 4
XA_HEAD_DIM = D_MODEL // XA_HEADS
D_FF = 2816
EPS = 1e-6
LR_PAD = 128

SEQ_TILE = 256
GATE_PIECE = 256
ROW_TILE = 1024
FFN_TILE = 256
FFN_PIPE_DEPTH = 2
FF_CHUNK = 256
CAST_ROWS = 256
CAST_BLOCK_BYTES = 6 * 1024 * 1024
VMEM_LIMIT = 56 * 1024 * 1024


def _dot(a, b):
    return jnp.dot(a, b, preferred_element_type=F32)


def _dot_nt(a, b):
    return lax.dot_general(a, b, (((1,), (1,)), ((), ())), preferred_element_type=F32)


def _rms(x, g):
    ms = jnp.mean(x * x, axis=-1, keepdims=True)
    return x * lax.rsqrt(ms + EPS) * g


def _gelu_tanh(x):
    return 0.5 * x * (1.0 + jnp.tanh(0.7978845608028654 * (x + 0.044715 * (x * x * x))))


def _softplus(x):
    return jnp.maximum(x, 0.0) + jnp.log1p(jnp.exp(-jnp.abs(x)))


def _shift_rows(z, tail8, k):
    zr = pltpu.roll(z, k, axis=0)
    cr = pltpu.roll(tail8, k, axis=0)
    row = lax.broadcasted_iota(jnp.int32, tail8.shape, 0)
    head = jnp.where(row < k, cr, zr[:8])
    return jnp.concatenate([head, zr[8:]], axis=0)


def _shift_rows_fill(z, k, fill):
    n, w = z.shape
    if k % 8 == 0:
        return jnp.concatenate([jnp.full((k, w), fill, z.dtype), z[: n - k]], axis=0)
    zr = pltpu.roll(z, k, axis=0)
    row = lax.broadcasted_iota(jnp.int32, (8, w), 0)
    head = jnp.where(row < k, jnp.full((8, w), fill, z.dtype), zr[:8])
    return jnp.concatenate([head, zr[8:]], axis=0)


def _hi_lo(x):
    hi = x.astype(BF)
    lo = (x - hi.astype(F32)).astype(BF)
    return hi, lo


def _mixer_masks():
    ts, w = SEQ_TILE, BRANCH_W
    r = np.arange(ts)[:, None]
    c = np.arange(ts)[None, :]
    same_chunk = (r // GLA_CHUNK) == (c // GLA_CHUNK)
    l_incl = (same_chunk & (c <= r)).astype(np.float32)
    s_mask = np.tile(l_incl, (N_GROUPS, 1))
    g = np.arange(w)
    bd = ((g[:, None] // GROUP_W) == (g[None, :] // GROUP_W)).astype(np.float32)
    i = np.arange(SGU_CHUNK)[:, None]
    j = np.arange(N_GROUPS * SGU_CHUNK)[None, :] % SGU_CHUNK
    w_mask = (j <= i).astype(np.float32)
    return (jnp.asarray(l_incl, BF), jnp.asarray(s_mask, BF), jnp.asarray(bd, F32),
            jnp.asarray(bd / GROUP_W, BF), jnp.asarray(w_mask, BF))


def _mixer_kernel(x_ref, g_ref, wab_ref, wcd_ref, wlr_ref, scw_ref, wal_ref, bal_ref, gng_ref,
                  lng_ref, lnb_ref, sgw_ref, sgb_ref, lcw_ref, lcb_ref, wax_ref, bax_ref, lam_ref,
                  wg_ref, bg_ref, wb_ref, wo_ref,
                  lincl_ref, smask_ref, bd_ref, bdmean_ref, wmask_ref,
                  out_ref, zc_ref, rc_ref, st_ref, hc_ref):
    ts = SEQ_TILE
    w = BRANCH_W

    @pl.when(pl.program_id(1) == 0)
    def _():
        zc_ref[...] = jnp.zeros_like(zc_ref)
        rc_ref[...] = jnp.zeros_like(rc_ref)
        st_ref[...] = jnp.zeros_like(st_ref)
        hc_ref[...] = jnp.zeros_like(hc_ref)

    x = x_ref[0]
    h = _rms(x, g_ref[0:1, :]).astype(BF)
    lane_grp = lax.broadcasted_iota(jnp.int32, (1, w), 1) >> 6
    n_chunks = ts // GLA_CHUNK
    gp = GATE_PIECE

    def proj(w_ref, c0, c1):
        return _dot(h, w_ref[:, c0:c1])

    def gate_piece(kk, nb):
        c0 = kk * D_MODEL + nb * gp
        return _dot(h, wg_ref[:, c0:c0 + gp])

    def gate_pre(kk):
        return [gate_piece(kk, nb) for nb in range(D_MODEL // gp)]

    def branch_out(kk, yk):
        return _dot(yk.astype(BF), wb_ref[kk])

    def gated(kk, zg_pieces, br):
        zg = jnp.concatenate(zg_pieces, axis=-1)
        return (jnp.tanh(zg + bg_ref[kk:kk + 1, :]) + 1.0) * br

    p_d = proj(wcd_ref, 2 * w, 4 * w)
    p_a = proj(wab_ref, 0, 3 * w)

    rx, rg = p_d[:, 0:w], p_d[:, w:2 * w]
    rc = rc_ref[...]
    lcw = lcw_ref[...]
    xc = (lcw[3:4] * rx + lcw[2:3] * _shift_rows(rx, rc, 1) + lcw[1:2] * _shift_rows(rx, rc, 2)
          + lcw[0:1] * _shift_rows(rx, rc, 3) + lcb_ref[...])
    rc_ref[...] = rx[ts - 8:]
    gates = jax.nn.sigmoid(_dot(xc.astype(BF), wax_ref[...]) + bax_ref[...])
    p_b = proj(wab_ref, 3 * w, 7 * w)
    p_lr = proj(wlr_ref, 0, LR_PAD)
    p_c = proj(wcd_ref, 0, 2 * w)
    log_a = (-LRU_C) * gates[:, 0:w] * _softplus(-lam_ref[...])
    acur = jnp.exp(log_a)
    t = jnp.tanh(log_a)
    bcur = jnp.sqrt(-2.0 * t / (1.0 - t)) * (gates[:, w:2 * w] * xc)
    s = 1
    while s < ts:
        a_s = _shift_rows_fill(acur, s, 1.0)
        b_s = _shift_rows_fill(bcur, s, 0.0)
        bcur = acur * b_s + bcur
        acur = acur * a_s
        s *= 2
    hseq = bcur + acur * hc_ref[7:8, :]
    hc_ref[...] = hseq[ts - 8:]
    br_d = branch_out(3, hseq * _gelu_tanh(rg))
    zg_d = gate_pre(3)

    z = p_a[:, w:2 * w] * p_a[:, 2 * w:3 * w]
    zc = zc_ref[...]
    scw = scw_ref[...]
    conv = scw[2:3] * z + scw[1:2] * _shift_rows(z, zc, 1) + scw[0:1] * _shift_rows(z, zc, 2)
    zc_ref[...] = z[ts - 8:]
    br_a = branch_out(0, p_a[:, 0:w] * conv)
    merged = gated(3, zg_d, br_d)

    q, k, v, r = p_b[:, 0:w], p_b[:, w:2 * w], p_b[:, 2 * w:3 * w], p_b[:, 3 * w:4 * w]
    logit = _dot(p_lr.astype(BF), wal_ref[...]) + bal_ref[...]
    glog = (jnp.minimum(logit, 0.0) - jnp.log1p(jnp.exp(-jnp.abs(logit)))) * (1.0 / GLA_TAU)
    g_hi, g_lo = _hi_lo(glog)
    zg_a = [gate_piece(0, 0)]
    l_incl = lincl_ref[...]
    gcum = _dot(l_incl, g_hi) + _dot(l_incl, g_lo)
    g_last = [gcum[(c + 1) * GLA_CHUNK - 1:(c + 1) * GLA_CHUNK] for c in range(n_chunks)]
    g_last_rows = jnp.concatenate([jnp.broadcast_to(gl, (GLA_CHUNK, w)) for gl in g_last], axis=0)
    qd_b = (q * (GROUP_W ** -0.5) * jnp.exp(gcum)).astype(BF)
    kd = (k * jnp.exp(-gcum)).astype(BF)
    kte = (k * jnp.exp(g_last_rows - gcum)).astype(BF)
    vb = v.astype(BF)
    vt = v.T.astype(BF)
    zero_b = jnp.zeros_like(qd_b)
    q_exp = jnp.concatenate([jnp.where(lane_grp == hh, qd_b, zero_b) for hh in range(N_GROUPS)], axis=0)
    zg_a += [gate_piece(0, 1), gate_piece(0, 2)]
    s_exp = _dot_nt(q_exp, kd).astype(BF)
    zero_rows = jnp.zeros((GLA_CHUNK, w), BF)
    kv_t = []
    for c in range(n_chunks):
        kte_c = jnp.concatenate([kte[c * GLA_CHUNK:(c + 1) * GLA_CHUNK] if cc == c else zero_rows
                                 for cc in range(n_chunks)], axis=0)
        kv_t.append(_dot(vt, kte_c))
    zg_a.append(gate_piece(0, 3))
    s_exp = jnp.where(smask_ref[...] != 0, s_exp, jnp.zeros_like(s_exp))
    o_exp = _dot(s_exp, vb)
    bd = bd_ref[...]
    st = st_ref[...]
    o_inter = []
    for c in range(n_chunks):
        o_inter.append(_dot_nt(qd_b[c * GLA_CHUNK:(c + 1) * GLA_CHUNK], st.astype(BF)))
        st = st * jnp.exp(g_last[c]) + bd * kv_t[c]
    st_ref[...] = st
    zg_b = [gate_piece(1, 0)]
    o = jnp.concatenate(o_inter, axis=0)
    for hh in range(N_GROUPS):
        o = o + jnp.where(lane_grp == hh, o_exp[hh * ts:(hh + 1) * ts], 0.0)
    ms = _dot((o * o).astype(BF), bdmean_ref[...])
    zg_b.append(gate_piece(1, 1))
    o = o * lax.rsqrt(ms + EPS) * gng_ref[...]
    br_b = branch_out(1, o * (r * jax.nn.sigmoid(r)))
    merged = merged + gated(0, zg_a, br_a)
    zg_b.append(gate_piece(1, 2))

    su, sv = p_c[:, 0:w], p_c[:, w:2 * w]
    mu = jnp.mean(sv, axis=-1, keepdims=True)
    svc = sv - mu
    var = jnp.mean(svc * svc, axis=-1, keepdims=True)
    vn = (svc * lax.rsqrt(var + EPS) * lng_ref[...] + lnb_ref[...]).astype(BF)
    w_mask = sgw_ref[...] * wmask_ref[...]
    zero_vn = jnp.zeros((SGU_CHUNK, w), BF)
    parts = []
    for c in range(ts // SGU_CHUNK):
        vc = vn[c * SGU_CHUNK:(c + 1) * SGU_CHUNK]
        v_exp = jnp.concatenate([jnp.where(lane_grp == gg, vc, zero_vn) for gg in range(N_GROUPS)], axis=0)
        mixed = _dot(w_mask, v_exp) + sgb_ref[...]
        parts.append(su[c * SGU_CHUNK:(c + 1) * SGU_CHUNK] * mixed)
    zg_b.append(gate_piece(1, 3))
    br_c = branch_out(2, jnp.concatenate(parts, axis=0))
    merged = merged + gated(1, zg_b, br_b)

    out = jnp.zeros(x.shape, F32)
    n_pieces = D_MODEL // gp
    zg_c = [gate_piece(2, 0)]
    for nb in range(n_pieces):
        if nb + 1 < n_pieces:
            zg_c.append(gate_piece(2, nb + 1))
        cols = slice(nb * gp, (nb + 1) * gp)
        m_nb = merged[:, cols] + (jnp.tanh(zg_c[nb] + bg_ref[2:3, cols]) + 1.0) * br_c[:, cols]
        out = out + _dot(m_nb.astype(BF), wo_ref[cols, :])
    out_ref[0] = x + _rms(out, g_ref[1:2, :])


def _const_spec(shape):
    nd = len(shape)
    return pl.BlockSpec(shape, lambda *_: (0,) * nd, pipeline_mode=pl.Buffered(1))


def _layer_spec(stacked, layer):
    shape = stacked.shape[1:]
    nd = len(shape)
    return pl.BlockSpec((None,) + shape, lambda *_: (layer,) + (0,) * nd, pipeline_mode=pl.Buffered(1))


def _mixer_call(x, layer, stacked, masks):
    b, s, d = x.shape
    ts = SEQ_TILE
    consts = tuple(stacked) + tuple(masks)
    const_specs = [_layer_spec(a, layer) for a in stacked] + [_const_spec(m.shape) for m in masks]
    return pl.pallas_call(
        _mixer_kernel,
        out_shape=jax.ShapeDtypeStruct((b, s, d), F32),
        grid=(b, s // ts),
        in_specs=[pl.BlockSpec((1, ts, d), lambda i, j: (i, j, 0))] + const_specs,
        out_specs=pl.BlockSpec((1, ts, d), lambda i, j: (i, j, 0)),
        scratch_shapes=[pltpu.VMEM((8, BRANCH_W), F32), pltpu.VMEM((8, BRANCH_W), F32),
                        pltpu.VMEM((BRANCH_W, BRANCH_W), F32), pltpu.VMEM((8, BRANCH_W), F32)],
        compiler_params=pltpu.CompilerParams(dimension_semantics=("arbitrary", "arbitrary"),
                                             vmem_limit_bytes=VMEM_LIMIT),
        name="mixer",
    )(x, *consts)


def _kv_kernel(mem_ref, g_ref, wkv_ref, kv_ref):
    mn = _rms(mem_ref[...], g_ref[4:5, :]).astype(BF)
    kv_ref[...] = _dot(mn, wkv_ref[...]).astype(kv_ref.dtype)


def _kv_call(mem2, norm_g, wkv, layer):
    m, d = mem2.shape
    n = wkv.shape[-1]
    tn = 512
    return pl.pallas_call(
        _kv_kernel,
        out_shape=jax.ShapeDtypeStruct((m, n), BF),
        grid=(n // tn,),
        in_specs=[_const_spec((m, d)), _layer_spec(norm_g, layer),
                  pl.BlockSpec((None, d, tn), lambda j: (layer, 0, j))],
        out_specs=pl.BlockSpec((m, tn), lambda j: (0, j)),
        compiler_params=pltpu.CompilerParams(dimension_semantics=("parallel",), vmem_limit_bytes=VMEM_LIMIT),
        name="xattn_kv",
    )(mem2, norm_g, wkv)


def _xattn_kernel(x_ref, kv_ref, g_ref, wq_ref, wo_ref, out_ref):
    x = x_ref[0]
    h = _rms(x, g_ref[2:3, :]).astype(BF)
    q = (_dot(h, wq_ref[...]) * (XA_HEAD_DIM ** -0.5)).astype(BF)
    kv = kv_ref[0]
    outs = []
    for hd in range(XA_HEADS):
        lo, hi = hd * XA_HEAD_DIM, (hd + 1) * XA_HEAD_DIM
        sc = _dot_nt(q[:, lo:hi], kv[:, lo:hi])
        e = jnp.exp(sc - jnp.max(sc, axis=-1, keepdims=True))
        l = jnp.sum(e, axis=-1, keepdims=True)
        outs.append(_dot(e.astype(BF), kv[:, D_MODEL + lo:D_MODEL + hi]) / l)
    o = jnp.concatenate(outs, axis=-1).astype(BF)
    y = _dot(o, wo_ref[...])
    out_ref[0] = x + _rms(y, g_ref[3:4, :])


def _xattn_call(x, kv, norm_g, wq, wo, layer):
    b, s, d = x.shape
    tm = ROW_TILE
    return pl.pallas_call(
        _xattn_kernel,
        out_shape=jax.ShapeDtypeStruct((b, s, d), F32),
        grid=(b, s // tm),
        in_specs=[pl.BlockSpec((1, tm, d), lambda i, j: (i, j, 0)),
                  pl.BlockSpec((1, N_MEM, 2 * d), lambda i, j: (i, 0, 0)),
                  _layer_spec(norm_g, layer), _layer_spec(wq, layer), _layer_spec(wo, layer)],
        out_specs=pl.BlockSpec((1, tm, d), lambda i, j: (i, j, 0)),
        compiler_params=pltpu.CompilerParams(dimension_semantics=("parallel", "parallel"),
                                             vmem_limit_bytes=VMEM_LIMIT),
        name="xattn",
    )(x, kv, norm_g, wq, wo)


def _ffn_kernel(x_ref, g_ref, wup_ref, cw_ref, cb_ref, wdn_ref, out_ref, carry_ref):
    tm = FFN_TILE
    fc = FF_CHUNK

    @pl.when(pl.program_id(1) == 0)
    def _():
        carry_ref[...] = jnp.zeros_like(carry_ref)

    x = x_ref[0]
    h = _rms(x, g_ref[5:6, :]).astype(BF)
    acc = jnp.zeros(x.shape, F32)

    def conv(u, col):
        tail = carry_ref[:, col:col + fc]
        cw = cw_ref[:, col:col + fc]
        y = (cw[2:3] * u + cw[1:2] * _shift_rows(u, tail, 1) + cw[0:1] * _shift_rows(u, tail, 2)
             + cb_ref[:, col:col + fc])
        carry_ref[:, col:col + fc] = u[tm - 8:]
        return y

    def up_g(c):
        return _dot(h, wup_ref[:, c * fc:(c + 1) * fc])

    def up_v(c):
        return _dot(h, wup_ref[:, D_FF + c * fc:D_FF + (c + 1) * fc])

    n_chunks = D_FF // fc
    depth = FFN_PIPE_DEPTH
    ups = [(up_g(c), up_v(c)) for c in range(depth)]
    for c in range(n_chunks):
        if c + depth < n_chunks:
            ups.append((up_g(c + depth), up_v(c + depth)))
        ug, uv = ups[c]
        yg = conv(ug, c * fc)
        inner = yg * (yg * yg * (0.7978845608028654 * 0.044715) + 0.7978845608028654)
        act = (yg * (jnp.tanh(inner) + 1.0) * conv(uv, D_FF + c * fc)).astype(BF)
        acc = acc + _dot(act, wdn_ref[c * fc:(c + 1) * fc, :])
    out_ref[0] = x + _rms(acc, g_ref[6:7, :])


def _ffn_call(x, layer, stacked):
    b, s, d = x.shape
    tm = FFN_TILE
    consts = tuple(stacked)
    const_specs = [_layer_spec(a, layer) for a in stacked]
    return pl.pallas_call(
        _ffn_kernel,
        out_shape=jax.ShapeDtypeStruct((b, s, d), F32),
        grid=(b, s // tm),
        in_specs=[pl.BlockSpec((1, tm, d), lambda i, j: (i, j, 0))] + const_specs,
        out_specs=pl.BlockSpec((1, tm, d), lambda i, j: (i, j, 0)),
        scratch_shapes=[pltpu.VMEM((8, 2 * D_FF), F32)],
        compiler_params=pltpu.CompilerParams(dimension_semantics=("arbitrary", "arbitrary"),
                                             vmem_limit_bytes=VMEM_LIMIT),
        name="conv_ffn",
    )(x, *consts)


def _cast_kernel(w_ref, o_ref, *, scale):
    w = w_ref[...]
    if scale != 1.0:
        w = w * scale
    o_ref[...] = w.astype(o_ref.dtype)


def _cast_bf16(w, scale=1.0):
    nl, r, cols = w.shape
    tr = max(t for t in range(16, r + 1, 16) if r % t == 0 and t * cols * 4 <= CAST_BLOCK_BYTES)
    return pl.pallas_call(
        lambda w_ref, o_ref: _cast_kernel(w_ref, o_ref, scale=scale),
        out_shape=jax.ShapeDtypeStruct((nl, r, cols), BF),
        grid=(nl, r // tr),
        in_specs=[pl.BlockSpec((1, tr, cols), lambda l, i: (l, i, 0))],
        out_specs=pl.BlockSpec((1, tr, cols), lambda l, i: (l, i, 0)),
        compiler_params=pltpu.CompilerParams(dimension_semantics=("parallel", "parallel"),
                                             vmem_limit_bytes=VMEM_LIMIT),
        name="cast_bf16",
    )(w)


def _split_kernel(w_ref, ab_ref, cd_ref, lr_ref):
    lr0 = 7 * BRANCH_W
    w = w_ref[0]
    ab_ref[0] = w[:, :lr0].astype(BF)
    cd_ref[0] = w[:, lr0 + GLA_RANK:].astype(BF)
    lane = lax.broadcasted_iota(jnp.int32, (w.shape[0], LR_PAD), 1)
    lr_ref[0] = jnp.where(lane < GLA_RANK, w[:, lr0:lr0 + LR_PAD], 0.0).astype(BF)


def _split_in_proj(w_in):
    nl, r, c = w_in.shape
    lr0 = 7 * BRANCH_W
    tr = CAST_ROWS
    widths = (lr0, c - lr0 - GLA_RANK, LR_PAD)
    return pl.pallas_call(
        _split_kernel,
        out_shape=[jax.ShapeDtypeStruct((nl, r, n), BF) for n in widths],
        grid=(nl, r // tr),
        in_specs=[pl.BlockSpec((1, tr, c), lambda l, i: (l, i, 0))],
        out_specs=[pl.BlockSpec((1, tr, n), lambda l, i: (l, i, 0)) for n in widths],
        compiler_params=pltpu.CompilerParams(dimension_semantics=("parallel", "parallel"),
                                             vmem_limit_bytes=VMEM_LIMIT),
        name="split_in_proj",
    )(w_in)


def _block_diag(wg):
    nl, g, c, _ = wg.shape
    eye = jnp.eye(g, dtype=wg.dtype)
    return jnp.einsum('lgio,gh->lgiho', wg, eye).reshape(nl, g * c, g * c)


def kernel(x, mem, norm_g, w_in, sc_conv_w, gla_w_alpha, gla_b_alpha, gla_norm_g, sgu_ln_g, sgu_ln_b, sgu_w, sgu_b, lru_conv_w, lru_conv_b, lru_w_a, lru_b_a, lru_w_x, lru_b_x, lru_lambda, w_gate, b_gate, w_branch, w_mix_out, xa_wq, xa_wkv, xa_wo, ffn_w_up, ffn_conv_w, ffn_conv_b, ffn_w_down):
    bsz, s, d = x.shape
    depth = norm_g.shape[0]
    mem2 = mem.reshape(bsz * N_MEM, d)
    rows = lambda a: a.reshape(depth, 1, -1)
    w_ab, w_cd, w_lr = _split_in_proj(w_in)
    wal = jnp.pad(gla_w_alpha, ((0, 0), (0, LR_PAD - GLA_RANK), (0, 0))).astype(BF)
    sgw = jnp.transpose(sgu_w, (0, 2, 1, 3)).reshape(depth, SGU_CHUNK, N_GROUPS * SGU_CHUNK).astype(BF)
    sgb = jnp.repeat(jnp.transpose(sgu_b, (0, 2, 1)), GROUP_W, axis=2)
    wax = jnp.concatenate([_block_diag(lru_w_a), _block_diag(lru_w_x)], axis=2).astype(BF)
    bax = jnp.concatenate([lru_b_a, lru_b_x], axis=1)
    mixer_ops = (
        norm_g, w_ab, w_cd, w_lr, sc_conv_w, wal, rows(gla_b_alpha), rows(gla_norm_g), rows(sgu_ln_g),
        rows(sgu_ln_b), sgw, sgb, lru_conv_w, rows(lru_conv_b), wax, rows(bax), rows(lru_lambda),
        _cast_bf16(w_gate, 0.5), 0.5 * b_gate,
        _cast_bf16(w_branch.reshape(depth, N_GROUPS * BRANCH_W, d), 0.5).reshape(depth, 4, BRANCH_W, d),
        _cast_bf16(w_mix_out))
    masks = _mixer_masks()
    w_q, w_kv, w_xo = _cast_bf16(xa_wq), _cast_bf16(xa_wkv), _cast_bf16(xa_wo)
    ffn_ops = (norm_g, _cast_bf16(ffn_w_up), ffn_conv_w, rows(ffn_conv_b), _cast_bf16(ffn_w_down, 0.5))
    for l in range(depth):
        x = _mixer_call(x, l, mixer_ops, masks)
        kv = _kv_call(mem2, norm_g, w_kv, l).reshape(bsz, N_MEM, 2 * d)
        x = _xattn_call(x, kv, norm_g, w_q, w_xo, l)
        x = _ffn_call(x, l, ffn_ops)
    return x
```

```python
import functools

import jax
import jax.numpy as jnp
import numpy as np
from jax import lax
from jax.experimental import pallas as pl
from jax.experimental.pallas import tpu as pltpu

BF = jnp.bfloat16
F32 = jnp.float32

D_MODEL = 1024
BRANCH_W = 256
N_GROUPS = 4
GROUP_W = 64
GLA_CHUNK = 64
GLA_RANK = 16
GLA_TAU = 16.0
SGU_CHUNK = 128
LRU_C = 8.0
N_MEM = 256
XA_HEADS = 4
XA_HEAD_DIM = D_MODEL // XA_HEADS
D_FF = 2816
EPS = 1e-6
LR_PAD = 128

SEQ_TILE = 256
GATE_PIECE = 256
ROW_TILE = 1024
FFN_TILE = 256
FFN_PIPE_DEPTH = 2
FF_CHUNK = 256
CAST_BLOCK_BYTES = 6 * 1024 * 1024
VMEM_LIMIT = 56 * 1024 * 1024


def _dot(a, b):
    return jnp.dot(a, b, preferred_element_type=F32)


def _dot_nt(a, b):
    return lax.dot_general(a, b, (((1,), (1,)), ((), ())), preferred_element_type=F32)


def _rms(x, g):
    ms = jnp.mean(x * x, axis=-1, keepdims=True)
    return x * lax.rsqrt(ms + EPS) * g


def _gelu_tanh(x):
    return 0.5 * x * (1.0 + jnp.tanh(0.7978845608028654 * (x + 0.044715 * (x * x * x))))


def _softplus(x):
    return jnp.maximum(x, 0.0) + jnp.log1p(jnp.exp(-jnp.abs(x)))


def _shift_rows(z, tail8, k):
    zr = pltpu.roll(z, k, axis=0)
    cr = pltpu.roll(tail8, k, axis=0)
    row = lax.broadcasted_iota(jnp.int32, tail8.shape, 0)
    head = jnp.where(row < k, cr, zr[:8])
    return jnp.concatenate([head, zr[8:]], axis=0)


def _shift_rows_fill(z, k, fill):
    n, w = z.shape
    if k % 8 == 0:
        return jnp.concatenate([jnp.full((k, w), fill, z.dtype), z[: n - k]], axis=0)
    zr = pltpu.roll(z, k, axis=0)
    row = lax.broadcasted_iota(jnp.int32, (8, w), 0)
    head = jnp.where(row < k, jnp.full((8, w), fill, z.dtype), zr[:8])
    return jnp.concatenate([head, zr[8:]], axis=0)


def _hi_lo(x):
    hi = x.astype(BF)
    lo = (x - hi.astype(F32)).astype(BF)
    return hi, lo


def _mixer_masks():
    ts, w = SEQ_TILE, BRANCH_W
    r = np.arange(ts)[:, None]
    c = np.arange(ts)[None, :]
    same_chunk = (r // GLA_CHUNK) == (c // GLA_CHUNK)
    l_incl = (same_chunk & (c <= r)).astype(np.float32)
    s_mask = np.tile(l_incl, (N_GROUPS, 1))
    g = np.arange(w)
    bd = ((g[:, None] // GROUP_W) == (g[None, :] // GROUP_W)).astype(np.float32)
    i = np.arange(SGU_CHUNK)[:, None]
    j = np.arange(N_GROUPS * SGU_CHUNK)[None, :] % SGU_CHUNK
    w_mask = (j <= i).astype(np.float32)
    return (jnp.asarray(l_incl, BF), jnp.asarray(s_mask, BF), jnp.asarray(bd, F32),
            jnp.asarray(bd / GROUP_W, BF), jnp.asarray(w_mask, BF))


def _mixer_kernel(x_ref, g_ref, wab_ref, wcd_ref, wlr_ref, scw_ref, wal_ref, bal_ref, gng_ref,
                  lng_ref, lnb_ref, sgw_ref, sgb_ref, lcw_ref, lcb_ref, wax_ref, bax_ref, lam_ref,
                  wg_ref, bg_ref, wb_ref, wo_ref,
                  lincl_ref, smask_ref, bd_ref, bdmean_ref, wmask_ref,
                  out_ref, zc_ref, rc_ref, st_ref, hc_ref):
    ts = SEQ_TILE
    w = BRANCH_W

    @pl.when(pl.program_id(1) == 0)
    def _():
        zc_ref[...] = jnp.zeros_like(zc_ref)
        rc_ref[...] = jnp.zeros_like(rc_ref)
        st_ref[...] = jnp.zeros_like(st_ref)
        hc_ref[...] = jnp.zeros_like(hc_ref)

    x = x_ref[0]
    h = _rms(x, g_ref[0:1, :]).astype(BF)
    lane_grp = lax.broadcasted_iota(jnp.int32, (1, w), 1) >> 6
    n_chunks = ts // GLA_CHUNK
    gp = GATE_PIECE

    def proj(w_ref, c0, c1):
        return _dot(h, w_ref[:, c0:c1])

    def gate_piece(kk, nb):
        c0 = kk * D_MODEL + nb * gp
        return _dot(h, wg_ref[:, c0:c0 + gp])

    def gate_pre(kk):
        return [gate_piece(kk, nb) for nb in range(D_MODEL // gp)]

    def branch_out(kk, yk):
        return _dot(yk.astype(BF), wb_ref[kk])

    def gated(kk, zg_pieces, br):
        zg = jnp.concatenate(zg_pieces, axis=-1)
        return (jnp.tanh(zg + bg_ref[kk:kk + 1, :]) + 1.0) * br

    p_d = proj(wcd_ref, 2 * w, 4 * w)
    p_a = proj(wab_ref, 0, 3 * w)

    rx, rg = p_d[:, 0:w], p_d[:, w:2 * w]
    rc = rc_ref[...]
    lcw = lcw_ref[...]
    xc = (lcw[3:4] * rx + lcw[2:3] * _shift_rows(rx, rc, 1) + lcw[1:2] * _shift_rows(rx, rc, 2)
          + lcw[0:1] * _shift_rows(rx, rc, 3) + lcb_ref[...])
    rc_ref[...] = rx[ts - 8:]
    gates = jax.nn.sigmoid(_dot(xc.astype(BF), wax_ref[...]) + bax_ref[...])
    p_b = proj(wab_ref, 3 * w, 7 * w)
    p_lr = proj(wlr_ref, 0, LR_PAD)
    p_c = proj(wcd_ref, 0, 2 * w)
    log_a = (-LRU_C) * gates[:, 0:w] * _softplus(-lam_ref[...])
    acur = jnp.exp(log_a)
    t = jnp.tanh(log_a)
    bcur = jnp.sqrt(-2.0 * t / (1.0 - t)) * (gates[:, w:2 * w] * xc)
    s = 1
    while s < ts:
        a_s = _shift_rows_fill(acur, s, 1.0)
        b_s = _shift_rows_fill(bcur, s, 0.0)
        bcur = acur * b_s + bcur
        acur = acur * a_s
        s *= 2
    hseq = bcur + acur * hc_ref[7:8, :]
    hc_ref[...] = hseq[ts - 8:]
    br_d = branch_out(3, hseq * _gelu_tanh(rg))
    zg_d = gate_pre(3)

    z = p_a[:, w:2 * w] * p_a[:, 2 * w:3 * w]
    zc = zc_ref[...]
    scw = scw_ref[...]
    conv = scw[2:3] * z + scw[1:2] * _shift_rows(z, zc, 1) + scw[0:1] * _shift_rows(z, zc, 2)
    zc_ref[...] = z[ts - 8:]
    br_a = branch_out(0, p_a[:, 0:w] * conv)
    merged = gated(3, zg_d, br_d)

    q, k, v, r = p_b[:, 0:w], p_b[:, w:2 * w], p_b[:, 2 * w:3 * w], p_b[:, 3 * w:4 * w]
    logit = _dot(p_lr.astype(BF), wal_ref[...]) + bal_ref[...]
    glog = (jnp.minimum(logit, 0.0) - jnp.log1p(jnp.exp(-jnp.abs(logit)))) * (1.0 / GLA_TAU)
    g_hi, g_lo = _hi_lo(glog)
    zg_a = [gate_piece(0, 0)]
    l_incl = lincl_ref[...]
    gcum = _dot(l_incl, g_hi) + _dot(l_incl, g_lo)
    g_last = [gcum[(c + 1) * GLA_CHUNK - 1:(c + 1) * GLA_CHUNK] for c in range(n_chunks)]
    g_last_rows = jnp.concatenate([jnp.broadcast_to(gl, (GLA_CHUNK, w)) for gl in g_last], axis=0)
    qd_b = (q * (GROUP_W ** -0.5) * jnp.exp(gcum)).astype(BF)
    kd = (k * jnp.exp(-gcum)).astype(BF)
    kte = (k * jnp.exp(g_last_rows - gcum)).astype(BF)
    vb = v.astype(BF)
    vt = v.T.astype(BF)
    zero_b = jnp.zeros_like(qd_b)
    q_exp = jnp.concatenate([jnp.where(lane_grp == hh, qd_b, zero_b) for hh in range(N_GROUPS)], axis=0)
    zg_a += [gate_piece(0, 1), gate_piece(0, 2)]
    s_exp = _dot_nt(q_exp, kd).astype(BF)
    zero_rows = jnp.zeros((GLA_CHUNK, w), BF)
    kv_t = []
    for c in range(n_chunks):
        kte_c = jnp.concatenate([kte[c * GLA_CHUNK:(c + 1) * GLA_CHUNK] if cc == c else zero_rows
                                 for cc in range(n_chunks)], axis=0)
        kv_t.append(_dot(vt, kte_c))
    zg_a.append(gate_piece(0, 3))
    s_exp = jnp.where(smask_ref[...] != 0, s_exp, jnp.zeros_like(s_exp))
    o_exp = _dot(s_exp, vb)
    bd = bd_ref[...]
    st = st_ref[...]
    o_inter = []
    for c in range(n_chunks):
        o_inter.append(_dot_nt(qd_b[c * GLA_CHUNK:(c + 1) * GLA_CHUNK], st.astype(BF)))
        st = st * jnp.exp(g_last[c]) + bd * kv_t[c]
    st_ref[...] = st
    zg_b = [gate_piece(1, 0)]
    o = jnp.concatenate(o_inter, axis=0)
    for hh in range(N_GROUPS):
        o = o + jnp.where(lane_grp == hh, o_exp[hh * ts:(hh + 1) * ts], 0.0)
    ms = _dot((o * o).astype(BF), bdmean_ref[...])
    zg_b.append(gate_piece(1, 1))
    o = o * lax.rsqrt(ms + EPS) * gng_ref[...]
    br_b = branch_out(1, o * (r * jax.nn.sigmoid(r)))
    merged = merged + gated(0, zg_a, br_a)
    zg_b.append(gate_piece(1, 2))

    su, sv = p_c[:, 0:w], p_c[:, w:2 * w]
    mu = jnp.mean(sv, axis=-1, keepdims=True)
    svc = sv - mu
    var = jnp.mean(svc * svc, axis=-1, keepdims=True)
    vn = (svc * lax.rsqrt(var + EPS) * lng_ref[...] + lnb_ref[...]).astype(BF)
    w_mask = sgw_ref[...] * wmask_ref[...]
    zero_vn = jnp.zeros((SGU_CHUNK, w), BF)
    parts = []
    for c in range(ts // SGU_CHUNK):
        vc = vn[c * SGU_CHUNK:(c + 1) * SGU_CHUNK]
        v_exp = jnp.concatenate([jnp.where(lane_grp == gg, vc, zero_vn) for gg in range(N_GROUPS)], axis=0)
        mixed = _dot(w_mask, v_exp) + sgb_ref[...]
        parts.append(su[c * SGU_CHUNK:(c + 1) * SGU_CHUNK] * mixed)
    zg_b.append(gate_piece(1, 3))
    br_c = branch_out(2, jnp.concatenate(parts, axis=0))
    merged = merged + gated(1, zg_b, br_b)

    out = jnp.zeros(x.shape, F32)
    n_pieces = D_MODEL // gp
    zg_c = [gate_piece(2, 0)]
    for nb in range(n_pieces):
        if nb + 1 < n_pieces:
            zg_c.append(gate_piece(2, nb + 1))
        cols = slice(nb * gp, (nb + 1) * gp)
        m_nb = merged[:, cols] + (jnp.tanh(zg_c[nb] + bg_ref[2:3, cols]) + 1.0) * br_c[:, cols]
        out = out + _dot(m_nb.astype(BF), wo_ref[cols, :])
    out_ref[0] = x + _rms(out, g_ref[1:2, :])


def _const_spec(shape):
    nd = len(shape)
    return pl.BlockSpec(shape, lambda *_: (0,) * nd, pipeline_mode=pl.Buffered(1))


def _layer_spec(stacked, layer):
    shape = stacked.shape[1:]
    nd = len(shape)
    return pl.BlockSpec((None,) + shape, lambda *_: (layer,) + (0,) * nd, pipeline_mode=pl.Buffered(1))


def _mixer_with_casts_kernel(n_main_in, scales, *refs):
    n = len(scales)
    main_in, side_in = refs[:n_main_in], refs[n_main_in:n_main_in + n]
    out_ref = refs[n_main_in + n]
    side_out = refs[n_main_in + n + 1:n_main_in + 2 * n + 1]
    scratch = refs[n_main_in + 2 * n + 1:]
    for w_ref, o_ref, scale in zip(side_in, side_out, scales):
        _cast_kernel(w_ref, o_ref, scale=scale)
    _mixer_kernel(*main_in, out_ref, *scratch)


def _mixer_call(x, layer, stacked, masks, side_casts=()):
    b, s, d = x.shape
    ts = SEQ_TILE
    nj = s // ts
    n_steps = b * nj
    consts = tuple(stacked) + tuple(masks)
    const_specs = [_layer_spec(a, layer) for a in stacked] + [_const_spec(m.shape) for m in masks]
    side_arrays, side_specs, side_shapes, scales = [], [], [], []
    for w2, scale in side_casts:
        r, c = w2.shape
        rows = next(t for t in range(16, r + 1, 16) if r % t == 0 and r // t <= n_steps)
        last = r // rows - 1
        spec = pl.BlockSpec((rows, c), lambda i, j, last=last: (jnp.minimum(i * nj + j, last), 0))
        side_arrays.append(w2)
        side_specs.append(spec)
        side_shapes.append(jax.ShapeDtypeStruct((r, c), BF))
        scales.append(scale)
    x_spec = pl.BlockSpec((1, ts, d), lambda i, j: (i, j, 0))
    if side_casts:
        body = functools.partial(_mixer_with_casts_kernel, 1 + len(consts), tuple(scales))
    else:
        body = _mixer_kernel
    outs = pl.pallas_call(
        body,
        out_shape=[jax.ShapeDtypeStruct((b, s, d), F32)] + side_shapes,
        grid=(b, nj),
        in_specs=[x_spec] + const_specs + side_specs,
        out_specs=[x_spec] + side_specs,
        scratch_shapes=[pltpu.VMEM((8, BRANCH_W), F32), pltpu.VMEM((8, BRANCH_W), F32),
                        pltpu.VMEM((BRANCH_W, BRANCH_W), F32), pltpu.VMEM((8, BRANCH_W), F32)],
        compiler_params=pltpu.CompilerParams(dimension_semantics=("arbitrary", "arbitrary"),
                                             vmem_limit_bytes=VMEM_LIMIT),
        name="mixer",
    )(x, *consts, *side_arrays)
    return outs[0], outs[1:]


def _kv_kernel(mem_ref, g_ref, wkv_ref, kv_ref):
    mn = _rms(mem_ref[...], g_ref[4:5, :]).astype(BF)
    kv_ref[...] = _dot(mn, wkv_ref[...]).astype(kv_ref.dtype)


def _kv_call(mem2, norm_g, wkv, layer):
    m, d = mem2.shape
    n = wkv.shape[-1]
    tn = 512
    return pl.pallas_call(
        _kv_kernel,
        out_shape=jax.ShapeDtypeStruct((m, n), BF),
        grid=(n // tn,),
        in_specs=[_const_spec((m, d)), _layer_spec(norm_g, layer),
                  pl.BlockSpec((None, d, tn), lambda j: (layer, 0, j))],
        out_specs=pl.BlockSpec((m, tn), lambda j: (0, j)),
        compiler_params=pltpu.CompilerParams(dimension_semantics=("parallel",), vmem_limit_bytes=VMEM_LIMIT),
        name="xattn_kv",
    )(mem2, norm_g, wkv)


def _xattn_kernel(x_ref, kv_ref, g_ref, wq_ref, wo_ref, out_ref):
    x = x_ref[0]
    h = _rms(x, g_ref[2:3, :]).astype(BF)
    q = (_dot(h, wq_ref[...]) * (XA_HEAD_DIM ** -0.5)).astype(BF)
    kv = kv_ref[0]
    outs = []
    for hd in range(XA_HEADS):
        lo, hi = hd * XA_HEAD_DIM, (hd + 1) * XA_HEAD_DIM
        sc = _dot_nt(q[:, lo:hi], kv[:, lo:hi])
        e = jnp.exp(sc - jnp.max(sc, axis=-1, keepdims=True))
        l = jnp.sum(e, axis=-1, keepdims=True)
        outs.append(_dot(e.astype(BF), kv[:, D_MODEL + lo:D_MODEL + hi]) / l)
    o = jnp.concatenate(outs, axis=-1).astype(BF)
    y = _dot(o, wo_ref[...])
    out_ref[0] = x + _rms(y, g_ref[3:4, :])


def _xattn_call(x, kv, norm_g, wq, wo, layer):
    b, s, d = x.shape
    tm = ROW_TILE
    return pl.pallas_call(
        _xattn_kernel,
        out_shape=jax.ShapeDtypeStruct((b, s, d), F32),
        grid=(b, s // tm),
        in_specs=[pl.BlockSpec((1, tm, d), lambda i, j: (i, j, 0)),
                  pl.BlockSpec((1, N_MEM, 2 * d), lambda i, j: (i, 0, 0)),
                  _layer_spec(norm_g, layer), _layer_spec(wq, layer), _layer_spec(wo, layer)],
        out_specs=pl.BlockSpec((1, tm, d), lambda i, j: (i, j, 0)),
        compiler_params=pltpu.CompilerParams(dimension_semantics=("parallel", "parallel"),
                                             vmem_limit_bytes=VMEM_LIMIT),
        name="xattn",
    )(x, kv, norm_g, wq, wo)


def _ffn_kernel(x_ref, g_ref, wup_ref, cw_ref, cb_ref, wdn_ref, out_ref, carry_ref):
    tm = FFN_TILE
    fc = FF_CHUNK

    @pl.when(pl.program_id(1) == 0)
    def _():
        carry_ref[...] = jnp.zeros_like(carry_ref)

    x = x_ref[0]
    h = _rms(x, g_ref[5:6, :]).astype(BF)
    acc = jnp.zeros(x.shape, F32)

    def conv(u, col):
        tail = carry_ref[:, col:col + fc]
        cw = cw_ref[:, col:col + fc]
        y = (cw[2:3] * u + cw[1:2] * _shift_rows(u, tail, 1) + cw[0:1] * _shift_rows(u, tail, 2)
             + cb_ref[:, col:col + fc])
        carry_ref[:, col:col + fc] = u[tm - 8:]
        return y

    def up_g(c):
        return _dot(h, wup_ref[:, c * fc:(c + 1) * fc])

    def up_v(c):
        return _dot(h, wup_ref[:, D_FF + c * fc:D_FF + (c + 1) * fc])

    n_chunks = D_FF // fc
    depth = FFN_PIPE_DEPTH
    ups = [(up_g(c), up_v(c)) for c in range(depth)]
    for c in range(n_chunks):
        if c + depth < n_chunks:
            ups.append((up_g(c + depth), up_v(c + depth)))
        ug, uv = ups[c]
        yg = conv(ug, c * fc)
        inner = yg * (yg * yg * (0.7978845608028654 * 0.044715) + 0.7978845608028654)
        act = (yg * (jnp.tanh(inner) + 1.0) * conv(uv, D_FF + c * fc)).astype(BF)
        acc = acc + _dot(act, wdn_ref[c * fc:(c + 1) * fc, :])
    out_ref[0] = x + _rms(acc, g_ref[6:7, :])


def _ffn_call(x, layer, stacked):
    b, s, d = x.shape
    tm = FFN_TILE
    consts = tuple(stacked)
    const_specs = [_layer_spec(a, layer) for a in stacked]
    return pl.pallas_call(
        _ffn_kernel,
        out_shape=jax.ShapeDtypeStruct((b, s, d), F32),
        grid=(b, s // tm),
        in_specs=[pl.BlockSpec((1, tm, d), lambda i, j: (i, j, 0))] + const_specs,
        out_specs=pl.BlockSpec((1, tm, d), lambda i, j: (i, j, 0)),
        scratch_shapes=[pltpu.VMEM((8, 2 * D_FF), F32)],
        compiler_params=pltpu.CompilerParams(dimension_semantics=("arbitrary", "arbitrary"),
                                             vmem_limit_bytes=VMEM_LIMIT),
        name="conv_ffn",
    )(x, *consts)


def _cast_kernel(w_ref, o_ref, *, scale):
    w = w_ref[...]
    if scale != 1.0:
        w = w * scale
    o_ref[...] = w.astype(o_ref.dtype)


def _cast_bf16(w, scale=1.0):
    nl, r, cols = w.shape
    tr = max(t for t in range(16, r + 1, 16) if r % t == 0 and t * cols * 4 <= CAST_BLOCK_BYTES)
    return pl.pallas_call(
        lambda w_ref, o_ref: _cast_kernel(w_ref, o_ref, scale=scale),
        out_shape=jax.ShapeDtypeStruct((nl, r, cols), BF),
        grid=(nl, r // tr),
        in_specs=[pl.BlockSpec((1, tr, cols), lambda l, i: (l, i, 0))],
        out_specs=pl.BlockSpec((1, tr, cols), lambda l, i: (l, i, 0)),
        compiler_params=pltpu.CompilerParams(dimension_semantics=("parallel", "parallel"),
                                             vmem_limit_bytes=VMEM_LIMIT),
        name="cast_bf16",
    )(w)


def _split_kernel(wt_ref, ab_ref, cd_ref, lr_ref):
    lr0 = 7 * BRANCH_W
    ab_ref[...] = wt_ref[0:lr0, :].T.astype(BF)
    cd_ref[...] = wt_ref[lr0 + GLA_RANK:, :].T.astype(BF)
    lr_rows = jnp.concatenate([wt_ref[lr0:lr0 + GLA_RANK, :], jnp.zeros((LR_PAD - GLA_RANK, D_MODEL), F32)], axis=0)
    lr_ref[...] = lr_rows.T.astype(BF)


def _split_in_proj(w_in):
    nl, r, c = w_in.shape
    lr0 = 7 * BRANCH_W
    widths = (lr0, c - lr0 - GLA_RANK, LR_PAD)
    return pl.pallas_call(
        _split_kernel,
        out_shape=[jax.ShapeDtypeStruct((nl, r, n), BF) for n in widths],
        grid=(nl,),
        in_specs=[pl.BlockSpec((None, c, r), lambda l: (l, 0, 0))],
        out_specs=[pl.BlockSpec((None, r, n), lambda l: (l, 0, 0)) for n in widths],
        compiler_params=pltpu.CompilerParams(dimension_semantics=("parallel",), vmem_limit_bytes=VMEM_LIMIT),
        name="split_in_proj",
    )(jnp.swapaxes(w_in, 1, 2))


def _block_diag(wg):
    nl, g, c, _ = wg.shape
    eye = jnp.eye(g, dtype=wg.dtype)
    return jnp.einsum('lgio,gh->lgiho', wg, eye).reshape(nl, g * c, g * c)


def kernel(x, mem, norm_g, w_in, sc_conv_w, gla_w_alpha, gla_b_alpha, gla_norm_g, sgu_ln_g, sgu_ln_b, sgu_w, sgu_b, lru_conv_w, lru_conv_b, lru_w_a, lru_b_a, lru_w_x, lru_b_x, lru_lambda, w_gate, b_gate, w_branch, w_mix_out, xa_wq, xa_wkv, xa_wo, ffn_w_up, ffn_conv_w, ffn_conv_b, ffn_w_down):
    bsz, s, d = x.shape
    depth = norm_g.shape[0]
    mem2 = mem.reshape(bsz * N_MEM, d)
    rows = lambda a: a.reshape(depth, 1, -1)
    w_ab, w_cd, w_lr = _split_in_proj(w_in)
    wal = jnp.pad(gla_w_alpha, ((0, 0), (0, LR_PAD - GLA_RANK), (0, 0))).astype(BF)
    sgw = jnp.transpose(sgu_w, (0, 2, 1, 3)).reshape(depth, SGU_CHUNK, N_GROUPS * SGU_CHUNK).astype(BF)
    sgb = jnp.repeat(jnp.transpose(sgu_b, (0, 2, 1)), GROUP_W, axis=2)
    wax = jnp.concatenate([_block_diag(lru_w_a), _block_diag(lru_w_x)], axis=2).astype(BF)
    bax = jnp.concatenate([lru_b_a, lru_b_x], axis=1)
    mixer_ops = (
        norm_g, w_ab, w_cd, w_lr, sc_conv_w, wal, rows(gla_b_alpha), rows(gla_norm_g), rows(sgu_ln_g),
        rows(sgu_ln_b), sgw, sgb, lru_conv_w, rows(lru_conv_b), wax, rows(bax), rows(lru_lambda),
        _cast_bf16(w_gate, 0.5), 0.5 * b_gate,
        _cast_bf16(w_branch.reshape(depth, N_GROUPS * BRANCH_W, d), 0.5).reshape(depth, 4, BRANCH_W, d),
        _cast_bf16(w_mix_out))
    masks = _mixer_masks()
    later = ((xa_wq, 1.0), (xa_wkv, 1.0), (xa_wo, 1.0), (ffn_w_up, 1.0), (ffn_w_down, 0.5))
    x, later_bf = _mixer_call(x, 0, mixer_ops, masks,
                              [(w.reshape(-1, w.shape[-1]), sc) for w, sc in later])
    w_q, w_kv, w_xo, w_up, w_dn = (wb.reshape(w.shape) for wb, (w, _) in zip(later_bf, later))
    ffn_ops = (norm_g, w_up, ffn_conv_w, rows(ffn_conv_b), w_dn)
    for l in range(depth):
        if l > 0:
            x, _ = _mixer_call(x, l, mixer_ops, masks)
        kv = _kv_call(mem2, norm_g, w_kv, l).reshape(bsz, N_MEM, 2 * d)
        x = _xattn_call(x, kv, norm_g, w_q, w_xo, l)
        x = _ffn_call(x, l, ffn_ops)
    return x
```

```python
import functools

import jax
import jax.numpy as jnp
import numpy as np
from jax import lax
from jax.experimental import pallas as pl
from jax.experimental.pallas import tpu as pltpu

BF = jnp.bfloat16
F32 = jnp.float32

D_MODEL = 1024
BRANCH_W = 256
N_GROUPS = 4
GROUP_W = 64
GROUP_SHIFT = GROUP_W.bit_length() - 1
GLA_CHUNK = 64
GLA_RANK = 16
GLA_TAU = 16.0
SGU_CHUNK = 128
LRU_C = 8.0
N_MEM = 256
XA_HEADS = 4
XA_HEAD_DIM = D_MODEL // XA_HEADS
D_FF = 2816
EPS = 1e-6
LR_PAD = 128

SEQ_TILE = 256
GATE_PIECE = 256
ROW_TILE = 1024
KV_COL_TILE = 512
FFN_TILE = 256
FFN_PIPE_DEPTH = 2
FF_CHUNK = 256
CAST_BLOCK_BYTES = 6 * 1024 * 1024
VMEM_LIMIT = 56 * 1024 * 1024


def _dot(a, b):
    return jnp.dot(a, b, preferred_element_type=F32)


def _dot_nt(a, b):
    return lax.dot_general(a, b, (((1,), (1,)), ((), ())), preferred_element_type=F32)


def _rms(x, g):
    ms = jnp.mean(x * x, axis=-1, keepdims=True)
    return x * lax.rsqrt(ms + EPS) * g


def _gelu_tanh(x):
    return 0.5 * x * (1.0 + jnp.tanh(0.7978845608028654 * (x + 0.044715 * (x * x * x))))


def _softplus(x):
    return jnp.maximum(x, 0.0) + jnp.log1p(jnp.exp(-jnp.abs(x)))


def _shift_rows(z, tail8, k):
    zr = pltpu.roll(z, k, axis=0)
    cr = pltpu.roll(tail8, k, axis=0)
    row = lax.broadcasted_iota(jnp.int32, tail8.shape, 0)
    head = jnp.where(row < k, cr, zr[:8])
    return jnp.concatenate([head, zr[8:]], axis=0)


def _linear_scan(a, b, h_init):
    n, w = a.shape
    row8 = jnp.concatenate([lax.broadcasted_iota(jnp.int32, (8, w), 0)] * (n // 8), axis=0)
    s = 1
    while s < 8:
        keep = row8 >= s
        a_s = jnp.where(keep, pltpu.roll(a, s, axis=0), 1.0)
        b_s = jnp.where(keep, pltpu.roll(b, s, axis=0), 0.0)
        b = a * b_s + b
        a = a * a_s
        s *= 2
    groups = []
    carry = h_init
    for j in range(n // 8):
        hg = b[8 * j:8 * j + 8] + a[8 * j:8 * j + 8] * carry
        groups.append(hg)
        carry = hg[7:8]
    return jnp.concatenate(groups, axis=0)


def _hi_lo(x):
    hi = x.astype(BF)
    lo = (x - hi.astype(F32)).astype(BF)
    return hi, lo


def _mixer_masks():
    ts, w = SEQ_TILE, BRANCH_W
    r = np.arange(ts)[:, None]
    c = np.arange(ts)[None, :]
    same_chunk = (r // GLA_CHUNK) == (c // GLA_CHUNK)
    l_incl = (same_chunk & (c <= r)).astype(np.float32)
    s_mask = np.tile(l_incl, (N_GROUPS, 1))
    g = np.arange(w)
    bd = ((g[:, None] // GROUP_W) == (g[None, :] // GROUP_W)).astype(np.float32)
    i = np.arange(SGU_CHUNK)[:, None]
    j = np.arange(N_GROUPS * SGU_CHUNK)[None, :] % SGU_CHUNK
    w_mask = (j <= i).astype(np.float32)
    return (jnp.asarray(l_incl, BF), jnp.asarray(s_mask, BF), jnp.asarray(bd, F32),
            jnp.asarray(bd / GROUP_W, BF), jnp.asarray(w_mask, BF))


def _mixer_kernel(x_ref, g_ref, wab_ref, wcd_ref, wlr_ref, scw_ref, wal_ref, bal_ref, gng_ref,
                  lng_ref, lnb_ref, sgw_ref, sgb_ref, lcw_ref, lcb_ref, wax_ref, bax_ref, lam_ref,
                  wg_ref, bg_ref, wb_ref, wo_ref,
                  lincl_ref, smask_ref, bd_ref, bdmean_ref, wmask_ref,
                  out_ref, zc_ref, rc_ref, st_ref, hc_ref):
    ts = SEQ_TILE
    w = BRANCH_W

    @pl.when(pl.program_id(1) == 0)
    def _():
        zc_ref[...] = jnp.zeros_like(zc_ref)
        rc_ref[...] = jnp.zeros_like(rc_ref)
        st_ref[...] = jnp.zeros_like(st_ref)
        hc_ref[...] = jnp.zeros_like(hc_ref)

    x = x_ref[0]
    h = _rms(x, g_ref[0:1, :]).astype(BF)
    lane_grp = lax.broadcasted_iota(jnp.int32, (1, w), 1) >> GROUP_SHIFT
    n_chunks = ts // GLA_CHUNK
    gp = GATE_PIECE

    def proj(w_ref, c0, c1):
        return _dot(h, w_ref[:, c0:c1])

    def gate_piece(kk, nb):
        c0 = kk * D_MODEL + nb * gp
        return _dot(h, wg_ref[:, c0:c0 + gp])

    def gate_pre(kk):
        return [gate_piece(kk, nb) for nb in range(D_MODEL // gp)]

    def branch_out(kk, yk):
        return _dot(yk.astype(BF), wb_ref[kk])

    def gated(kk, zg_pieces, br):
        zg = jnp.concatenate(zg_pieces, axis=-1)
        return (jnp.tanh(zg + bg_ref[kk:kk + 1, :]) + 1.0) * br

    p_d = proj(wcd_ref, 2 * w, 4 * w)
    p_a = proj(wab_ref, 0, 3 * w)

    rx, rg = p_d[:, 0:w], p_d[:, w:2 * w]
    rc = rc_ref[...]
    lcw = lcw_ref[...]
    xc = (lcw[3:4] * rx + lcw[2:3] * _shift_rows(rx, rc, 1) + lcw[1:2] * _shift_rows(rx, rc, 2)
          + lcw[0:1] * _shift_rows(rx, rc, 3) + lcb_ref[...])
    rc_ref[...] = rx[ts - 8:]
    gates = jax.nn.sigmoid(_dot(xc.astype(BF), wax_ref[...]) + bax_ref[...])
    p_b = proj(wab_ref, 3 * w, 7 * w)
    p_lr = proj(wlr_ref, 0, LR_PAD)
    p_c = proj(wcd_ref, 0, 2 * w)
    log_a = (-LRU_C) * gates[:, 0:w] * _softplus(-lam_ref[...])
    acur = jnp.exp(log_a)
    t = jnp.tanh(log_a)
    bcur = jnp.sqrt(-2.0 * t / (1.0 - t)) * (gates[:, w:2 * w] * xc)
    hseq = _linear_scan(acur, bcur, hc_ref[7:8, :])
    hc_ref[...] = hseq[ts - 8:]
    br_d = branch_out(3, hseq * _gelu_tanh(rg))
    zg_d = gate_pre(3)

    z = p_a[:, w:2 * w] * p_a[:, 2 * w:3 * w]
    zc = zc_ref[...]
    scw = scw_ref[...]
    conv = scw[2:3] * z + scw[1:2] * _shift_rows(z, zc, 1) + scw[0:1] * _shift_rows(z, zc, 2)
    zc_ref[...] = z[ts - 8:]
    br_a = branch_out(0, p_a[:, 0:w] * conv)
    merged = gated(3, zg_d, br_d)

    q, k, v, r = p_b[:, 0:w], p_b[:, w:2 * w], p_b[:, 2 * w:3 * w], p_b[:, 3 * w:4 * w]
    logit = _dot(p_lr.astype(BF), wal_ref[...]) + bal_ref[...]
    glog = (jnp.minimum(logit, 0.0) - jnp.log1p(jnp.exp(-jnp.abs(logit)))) * (1.0 / GLA_TAU)
    g_hi, g_lo = _hi_lo(glog)
    zg_a = [gate_piece(0, 0)]
    l_incl = lincl_ref[...]
    gcum = _dot(l_incl, g_hi) + _dot(l_incl, g_lo)
    g_last = [gcum[(c + 1) * GLA_CHUNK - 1:(c + 1) * GLA_CHUNK] for c in range(n_chunks)]
    g_last_rows = jnp.concatenate([jnp.broadcast_to(gl, (GLA_CHUNK, w)) for gl in g_last], axis=0)
    qd_b = (q * (GROUP_W ** -0.5) * jnp.exp(gcum)).astype(BF)
    kd = (k * jnp.exp(-gcum)).astype(BF)
    kte = (k * jnp.exp(g_last_rows - gcum)).astype(BF)
    vb = v.astype(BF)
    vt = v.T.astype(BF)
    zero_b = jnp.zeros_like(qd_b)
    q_exp = jnp.concatenate([jnp.where(lane_grp == hh, qd_b, zero_b) for hh in range(N_GROUPS)], axis=0)
    zg_a += [gate_piece(0, 1), gate_piece(0, 2)]
    s_exp = _dot_nt(q_exp, kd).astype(BF)
    zero_rows = jnp.zeros((GLA_CHUNK, w), BF)
    kv_t = []
    for c in range(n_chunks):
        kte_c = jnp.concatenate([kte[c * GLA_CHUNK:(c + 1) * GLA_CHUNK] if cc == c else zero_rows
                                 for cc in range(n_chunks)], axis=0)
        kv_t.append(_dot(vt, kte_c))
    zg_a.append(gate_piece(0, 3))
    s_exp = jnp.where(smask_ref[...] != 0, s_exp, jnp.zeros_like(s_exp))
    o_exp = _dot(s_exp, vb)
    bd = bd_ref[...]
    st = st_ref[...]
    o_inter = []
    for c in range(n_chunks):
        o_inter.append(_dot_nt(qd_b[c * GLA_CHUNK:(c + 1) * GLA_CHUNK], st.astype(BF)))
        st = st * jnp.exp(g_last[c]) + bd * kv_t[c]
    st_ref[...] = st
    zg_b = [gate_piece(1, 0)]
    o = jnp.concatenate(o_inter, axis=0)
    for hh in range(N_GROUPS):
        o = o + jnp.where(lane_grp == hh, o_exp[hh * ts:(hh + 1) * ts], 0.0)
    ms = _dot((o * o).astype(BF), bdmean_ref[...])
    zg_b.append(gate_piece(1, 1))
    o = o * lax.rsqrt(ms + EPS) * gng_ref[...]
    br_b = branch_out(1, o * (r * jax.nn.sigmoid(r)))
    merged = merged + gated(0, zg_a, br_a)
    zg_b.append(gate_piece(1, 2))

    su, sv = p_c[:, 0:w], p_c[:, w:2 * w]
    mu = jnp.mean(sv, axis=-1, keepdims=True)
    svc = sv - mu
    var = jnp.mean(svc * svc, axis=-1, keepdims=True)
    vn = (svc * lax.rsqrt(var + EPS) * lng_ref[...] + lnb_ref[...]).astype(BF)
    w_mask = sgw_ref[...] * wmask_ref[...]
    zero_vn = jnp.zeros((SGU_CHUNK, w), BF)
    parts = []
    for c in range(ts // SGU_CHUNK):
        vc = vn[c * SGU_CHUNK:(c + 1) * SGU_CHUNK]
        v_exp = jnp.concatenate([jnp.where(lane_grp == gg, vc, zero_vn) for gg in range(N_GROUPS)], axis=0)
        mixed = _dot(w_mask, v_exp) + sgb_ref[...]
        parts.append(su[c * SGU_CHUNK:(c + 1) * SGU_CHUNK] * mixed)
    zg_b.append(gate_piece(1, 3))
    br_c = branch_out(2, jnp.concatenate(parts, axis=0))
    merged = merged + gated(1, zg_b, br_b)

    out = jnp.zeros(x.shape, F32)
    n_pieces = D_MODEL // gp
    zg_c = [gate_piece(2, 0)]
    for nb in range(n_pieces):
        if nb + 1 < n_pieces:
            zg_c.append(gate_piece(2, nb + 1))
        cols = slice(nb * gp, (nb + 1) * gp)
        m_nb = merged[:, cols] + (jnp.tanh(zg_c[nb] + bg_ref[2:3, cols]) + 1.0) * br_c[:, cols]
        out = out + _dot(m_nb.astype(BF), wo_ref[cols, :])
    out_ref[0] = x + _rms(out, g_ref[1:2, :])


def _const_spec(shape):
    nd = len(shape)
    return pl.BlockSpec(shape, lambda *_: (0,) * nd, pipeline_mode=pl.Buffered(1))


def _layer_spec(stacked, layer):
    shape = stacked.shape[1:]
    nd = len(shape)
    return pl.BlockSpec((None,) + shape, lambda *_: (layer,) + (0,) * nd, pipeline_mode=pl.Buffered(1))


def _mixer_with_casts_kernel(n_main_in, scales, *refs):
    n = len(scales)
    main_in, side_in = refs[:n_main_in], refs[n_main_in:n_main_in + n]
    out_ref = refs[n_main_in + n]
    side_out = refs[n_main_in + n + 1:n_main_in + 2 * n + 1]
    scratch = refs[n_main_in + 2 * n + 1:]
    for w_ref, o_ref, scale in zip(side_in, side_out, scales):
        _cast_kernel(w_ref, o_ref, scale=scale)
    _mixer_kernel(*main_in, out_ref, *scratch)


def _mixer_call(x, layer, stacked, masks, side_casts=()):
    b, s, d = x.shape
    ts = SEQ_TILE
    nj = s // ts
    n_steps = b * nj
    consts = tuple(stacked) + tuple(masks)
    const_specs = [_layer_spec(a, layer) for a in stacked] + [_const_spec(m.shape) for m in masks]
    side_arrays, side_specs, side_shapes, scales = [], [], [], []
    for w2, scale in side_casts:
        r, c = w2.shape
        rows = next(t for t in range(16, r + 1, 16) if r % t == 0 and r // t <= n_steps)
        last = r // rows - 1
        spec = pl.BlockSpec((rows, c), lambda i, j, last=last: (jnp.minimum(i * nj + j, last), 0))
        side_arrays.append(w2)
        side_specs.append(spec)
        side_shapes.append(jax.ShapeDtypeStruct((r, c), BF))
        scales.append(scale)
    x_spec = pl.BlockSpec((1, ts, d), lambda i, j: (i, j, 0))
    if side_casts:
        body = functools.partial(_mixer_with_casts_kernel, 1 + len(consts), tuple(scales))
    else:
        body = _mixer_kernel
    outs = pl.pallas_call(
        body,
        out_shape=[jax.ShapeDtypeStruct((b, s, d), F32)] + side_shapes,
        grid=(b, nj),
        in_specs=[x_spec] + const_specs + side_specs,
        out_specs=[x_spec] + side_specs,
        scratch_shapes=[pltpu.VMEM((8, BRANCH_W), F32), pltpu.VMEM((8, BRANCH_W), F32),
                        pltpu.VMEM((BRANCH_W, BRANCH_W), F32), pltpu.VMEM((8, BRANCH_W), F32)],
        compiler_params=pltpu.CompilerParams(dimension_semantics=("arbitrary", "arbitrary"),
                                             vmem_limit_bytes=VMEM_LIMIT),
        name="mixer",
    )(x, *consts, *side_arrays)
    return outs[0], outs[1:]


def _kv_kernel(mem_ref, g_ref, wkv_ref, kv_ref):
    mn = _rms(mem_ref[...], g_ref[4:5, :]).astype(BF)
    kv_ref[...] = _dot(mn, wkv_ref[...]).astype(kv_ref.dtype)


def _kv_call(mem2, norm_g, wkv):
    m, d = mem2.shape
    nl, _, n = wkv.shape
    tn = KV_COL_TILE
    return pl.pallas_call(
        _kv_kernel,
        out_shape=jax.ShapeDtypeStruct((nl, m, n), BF),
        grid=(nl, n // tn),
        in_specs=[_const_spec((m, d)),
                  pl.BlockSpec((None,) + norm_g.shape[1:], lambda l, j: (l, 0, 0)),
                  pl.BlockSpec((None, d, tn), lambda l, j: (l, 0, j))],
        out_specs=pl.BlockSpec((None, m, tn), lambda l, j: (l, 0, j)),
        compiler_params=pltpu.CompilerParams(dimension_semantics=("parallel", "parallel"),
                                             vmem_limit_bytes=VMEM_LIMIT),
        name="xattn_kv",
    )(mem2, norm_g, wkv)


def _xattn_kernel(x_ref, kv_ref, g_ref, wq_ref, wo_ref, out_ref):
    x = x_ref[0]
    h = _rms(x, g_ref[2:3, :]).astype(BF)
    q = (_dot(h, wq_ref[...]) * (XA_HEAD_DIM ** -0.5)).astype(BF)
    kv = kv_ref[...]
    outs = []
    for hd in range(XA_HEADS):
        lo, hi = hd * XA_HEAD_DIM, (hd + 1) * XA_HEAD_DIM
        sc = _dot_nt(q[:, lo:hi], kv[:, lo:hi])
        e = jnp.exp(sc - jnp.max(sc, axis=-1, keepdims=True))
        l = jnp.sum(e, axis=-1, keepdims=True)
        outs.append(_dot(e.astype(BF), kv[:, D_MODEL + lo:D_MODEL + hi]) / l)
    o = jnp.concatenate(outs, axis=-1).astype(BF)
    y = _dot(o, wo_ref[...])
    out_ref[0] = x + _rms(y, g_ref[3:4, :])


def _xattn_call(x, kv, norm_g, wq, wo, layer):
    b, s, d = x.shape
    tm = ROW_TILE
    return pl.pallas_call(
        _xattn_kernel,
        out_shape=jax.ShapeDtypeStruct((b, s, d), F32),
        grid=(b, s // tm),
        in_specs=[pl.BlockSpec((1, tm, d), lambda i, j: (i, j, 0)),
                  pl.BlockSpec((None, None, N_MEM, 2 * d), lambda i, j: (layer, i, 0, 0)),
                  _layer_spec(norm_g, layer), _layer_spec(wq, layer), _layer_spec(wo, layer)],
        out_specs=pl.BlockSpec((1, tm, d), lambda i, j: (i, j, 0)),
        compiler_params=pltpu.CompilerParams(dimension_semantics=("parallel", "parallel"),
                                             vmem_limit_bytes=VMEM_LIMIT),
        name="xattn",
    )(x, kv, norm_g, wq, wo)


def _ffn_kernel(x_ref, g_ref, wup_ref, cw_ref, cb_ref, wdn_ref, out_ref, carry_ref):
    tm = FFN_TILE
    fc = FF_CHUNK

    @pl.when(pl.program_id(1) == 0)
    def _():
        carry_ref[...] = jnp.zeros_like(carry_ref)

    x = x_ref[0]
    h = _rms(x, g_ref[5:6, :]).astype(BF)
    acc = jnp.zeros(x.shape, F32)

    def conv(u, col):
        tail = carry_ref[:, col:col + fc]
        cw = cw_ref[:, col:col + fc]
        y = (cw[2:3] * u + cw[1:2] * _shift_rows(u, tail, 1) + cw[0:1] * _shift_rows(u, tail, 2)
             + cb_ref[:, col:col + fc])
        carry_ref[:, col:col + fc] = u[tm - 8:]
        return y

    def up_g(c):
        return _dot(h, wup_ref[:, c * fc:(c + 1) * fc])

    def up_v(c):
        return _dot(h, wup_ref[:, D_FF + c * fc:D_FF + (c + 1) * fc])

    n_chunks = D_FF // fc
    depth = FFN_PIPE_DEPTH
    ups = [(up_g(c), up_v(c)) for c in range(depth)]
    for c in range(n_chunks):
        if c + depth < n_chunks:
            ups.append((up_g(c + depth), up_v(c + depth)))
        ug, uv = ups[c]
        yg = conv(ug, c * fc)
        inner = yg * (yg * yg * (0.7978845608028654 * 0.044715) + 0.7978845608028654)
        act = (yg * (jnp.tanh(inner) + 1.0) * conv(uv, D_FF + c * fc)).astype(BF)
        acc = acc + _dot(act, wdn_ref[c * fc:(c + 1) * fc, :])
    out_ref[0] = x + _rms(acc, g_ref[6:7, :])


def _ffn_call(x, layer, stacked):
    b, s, d = x.shape
    tm = FFN_TILE
    consts = tuple(stacked)
    const_specs = [_layer_spec(a, layer) for a in stacked]
    return pl.pallas_call(
        _ffn_kernel,
        out_shape=jax.ShapeDtypeStruct((b, s, d), F32),
        grid=(b, s // tm),
        in_specs=[pl.BlockSpec((1, tm, d), lambda i, j: (i, j, 0))] + const_specs,
        out_specs=pl.BlockSpec((1, tm, d), lambda i, j: (i, j, 0)),
        scratch_shapes=[pltpu.VMEM((8, 2 * D_FF), F32)],
        compiler_params=pltpu.CompilerParams(dimension_semantics=("arbitrary", "arbitrary"),
                                             vmem_limit_bytes=VMEM_LIMIT),
        name="conv_ffn",
    )(x, *consts)


def _cast_kernel(w_ref, o_ref, *, scale):
    w = w_ref[...]
    if scale != 1.0:
        w = w * scale
    o_ref[...] = w.astype(o_ref.dtype)


def _cast_bf16(w, scale=1.0):
    nl, r, cols = w.shape
    tr = max(t for t in range(16, r + 1, 16) if r % t == 0 and t * cols * 4 <= CAST_BLOCK_BYTES)
    return pl.pallas_call(
        lambda w_ref, o_ref: _cast_kernel(w_ref, o_ref, scale=scale),
        out_shape=jax.ShapeDtypeStruct((nl, r, cols), BF),
        grid=(nl, r // tr),
        in_specs=[pl.BlockSpec((1, tr, cols), lambda l, i: (l, i, 0))],
        out_specs=pl.BlockSpec((1, tr, cols), lambda l, i: (l, i, 0)),
        compiler_params=pltpu.CompilerParams(dimension_semantics=("parallel", "parallel"),
                                             vmem_limit_bytes=VMEM_LIMIT),
        name="cast_bf16",
    )(w)


def _split_kernel(wt_ref, ab_ref, cd_ref, lr_ref):
    lr0 = 7 * BRANCH_W
    ab_ref[...] = wt_ref[0:lr0, :].T.astype(BF)
    cd_ref[...] = wt_ref[lr0 + GLA_RANK:, :].T.astype(BF)
    lr_rows = jnp.concatenate([wt_ref[lr0:lr0 + GLA_RANK, :], jnp.zeros((LR_PAD - GLA_RANK, D_MODEL), F32)], axis=0)
    lr_ref[...] = lr_rows.T.astype(BF)


def _split_in_proj(w_in):
    nl, r, c = w_in.shape
    lr0 = 7 * BRANCH_W
    widths = (lr0, c - lr0 - GLA_RANK, LR_PAD)
    return pl.pallas_call(
        _split_kernel,
        out_shape=[jax.ShapeDtypeStruct((nl, r, n), BF) for n in widths],
        grid=(nl,),
        in_specs=[pl.BlockSpec((None, c, r), lambda l: (l, 0, 0))],
        out_specs=[pl.BlockSpec((None, r, n), lambda l: (l, 0, 0)) for n in widths],
        compiler_params=pltpu.CompilerParams(dimension_semantics=("parallel",), vmem_limit_bytes=VMEM_LIMIT),
        name="split_in_proj",
    )(jnp.swapaxes(w_in, 1, 2))


def _block_diag(wg):
    nl, g, c, _ = wg.shape
    eye = jnp.eye(g, dtype=wg.dtype)
    return jnp.einsum('lgio,gh->lgiho', wg, eye).reshape(nl, g * c, g * c)


def kernel(x, mem, norm_g, w_in, sc_conv_w, gla_w_alpha, gla_b_alpha, gla_norm_g, sgu_ln_g, sgu_ln_b, sgu_w, sgu_b, lru_conv_w, lru_conv_b, lru_w_a, lru_b_a, lru_w_x, lru_b_x, lru_lambda, w_gate, b_gate, w_branch, w_mix_out, xa_wq, xa_wkv, xa_wo, ffn_w_up, ffn_conv_w, ffn_conv_b, ffn_w_down):
    bsz, s, d = x.shape
    depth = norm_g.shape[0]
    mem2 = mem.reshape(bsz * N_MEM, d)
    rows = lambda a: a.reshape(depth, 1, -1)
    w_ab, w_cd, w_lr = _split_in_proj(w_in)
    wal = jnp.pad(gla_w_alpha, ((0, 0), (0, LR_PAD - GLA_RANK), (0, 0))).astype(BF)
    sgw = jnp.transpose(sgu_w, (0, 2, 1, 3)).reshape(depth, SGU_CHUNK, N_GROUPS * SGU_CHUNK).astype(BF)
    sgb = jnp.repeat(jnp.transpose(sgu_b, (0, 2, 1)), GROUP_W, axis=2)
    wax = jnp.concatenate([_block_diag(lru_w_a), _block_diag(lru_w_x)], axis=2).astype(BF)
    bax = jnp.concatenate([lru_b_a, lru_b_x], axis=1)
    mixer_ops = (
        norm_g, w_ab, w_cd, w_lr, sc_conv_w, wal, rows(gla_b_alpha), rows(gla_norm_g), rows(sgu_ln_g),
        rows(sgu_ln_b), sgw, sgb, lru_conv_w, rows(lru_conv_b), wax, rows(bax), rows(lru_lambda),
        _cast_bf16(w_gate, 0.5), 0.5 * b_gate,
        _cast_bf16(w_branch.reshape(depth, N_GROUPS * BRANCH_W, d), 0.5).reshape(depth, N_GROUPS, BRANCH_W, d),
        _cast_bf16(w_mix_out))
    masks = _mixer_masks()
    later = ((xa_wq, 1.0), (xa_wkv, 1.0), (xa_wo, 1.0), (ffn_w_up, 1.0), (ffn_w_down, 0.5))
    x, later_bf = _mixer_call(x, 0, mixer_ops, masks,
                              [(w.reshape(-1, w.shape[-1]), sc) for w, sc in later])
    w_q, w_kv, w_xo, w_up, w_dn = (wb.reshape(w.shape) for wb, (w, _) in zip(later_bf, later))
    ffn_ops = (norm_g, w_up, ffn_conv_w, rows(ffn_conv_b), w_dn)
    kv = _kv_call(mem2, norm_g, w_kv).reshape(depth, bsz, N_MEM, 2 * d)
    for l in range(depth):
        if l > 0:
            x, _ = _mixer_call(x, l, mixer_ops, masks)
        x = _xattn_call(x, kv, norm_g, w_q, w_xo, l)
        x = _ffn_call(x, l, ffn_ops)
    return x
```
